```python
import math, functools
import jax, jax.numpy as jnp
from jax import lax
import numpy as np

D_MODEL = 4096
BATCH = 1
SEQ = 16384
DEPTH = 4

HEAD_DIM = 128
N_HEADS_MOBA = (D_MODEL // 2) // HEAD_DIM
SB_HEAD_DIM = 256
N_HEADS_SB = (D_MODEL // 2) // SB_HEAD_DIM
MOBA_BLOCK = 256
MOBA_TOPK = 3
MOBA_QG = 128
SB_Q_BLOCK = 128
SB_SPAN = 1024
RWKV_WIDTH = D_MODEL // 2
RWKV_HEAD = 64
RWKV_HEADS = RWKV_WIDTH // RWKV_HEAD
RWKV_CHUNK_LOG2 = 4
RWKV_CHUNK = 2 ** RWKV_CHUNK_LOG2
DECAY_LORA = max(32, int(round(1.8 * RWKV_WIDTH ** 0.5 / 32)) * 32)
AAA_LORA = max(32, int(round(1.8 * RWKV_WIDTH ** 0.5 / 32)) * 32)
MV_LORA = max(32, int(round(1.3 * RWKV_WIDTH ** 0.5 / 32)) * 32)
GATE_LORA = max(32, int(round(0.6 * RWKV_WIDTH ** 0.8 / 32)) * 32)
LRU_WIDTH = D_MODEL - RWKV_WIDTH
LRU_BLOCKS = 16
LRU_CONV = 4
LRU_C = 8.0
D_FF = 2 * D_MODEL
FFN_CONV = 3
FIRST_RWKV_LAYER = 1
RMS_EPS = 1e-6
GN_EPS = 64e-5

kernel_name = 'hybrid_moba_stickbreak_rwkv7_rglru_convffn'


def rms_norm(x, g):
    xf = x.astype(jnp.float32)
    y = xf * lax.rsqrt(jnp.mean(xf * xf, axis=-1, keepdims=True) + RMS_EPS)
    return (y * g.astype(jnp.float32)).astype(x.dtype)


def causal_dwconv(x, w):
    width, s = w.shape[0], x.shape[1]
    xp = jnp.pad(x, ((0, 0), (width - 1, 0), (0, 0)))
    return sum(xp[:, k:k + s] * w[k] for k in range(width))


def token_shift(x):
    return jnp.pad(x, ((0, 0), (1, 0), (0, 0)))[:, :-1]


def split_heads(x, n_heads):
    b, s, _ = x.shape
    return x.reshape(b, s, n_heads, -1).transpose(0, 2, 1, 3)


def merge_heads(x):
    b, h, s, d = x.shape
    return x.transpose(0, 2, 1, 3).reshape(b, s, h * d)


def alibi_slopes(n_heads):
    return 2.0 ** (-8.0 * jnp.arange(1, n_heads + 1, dtype=jnp.float32) / n_heads)


def moba_head(q, k, v, slope):
    f32 = jnp.float32
    q, k, v = q.astype(f32), k.astype(f32), v.astype(f32)
    s, dh = q.shape
    n_blk = -(-s // MOBA_BLOCK)
    pad = n_blk * MOBA_BLOCK - s
    to_blocks = lambda t: jnp.pad(t, ((0, pad), (0, 0))).reshape(n_blk, MOBA_BLOCK, dh)
    qb, kb, vb = to_blocks(q), to_blocks(k), to_blocks(v)
    scale = dh ** -0.5
    offs = jnp.arange(MOBA_BLOCK, dtype=jnp.int32)

    rel = offs[:, None] - offs[None, :]
    lo = jnp.einsum('nqd,nkd->nqk', qb, kb) * scale - slope * rel.astype(f32)
    lo = jnp.where(rel >= 0, lo, -jnp.inf)
    m_own = jnp.max(lo, axis=-1)
    p_own = jnp.exp(lo - m_own[..., None])
    l_own = jnp.sum(p_own, axis=-1).reshape(-1)[:s]
    acc_own = jnp.einsum('nqk,nkd->nqd', p_own, vb).reshape(-1, dh)[:s]
    m_own = m_own.reshape(-1)[:s]

    t = jnp.arange(s, dtype=jnp.int32)
    own = t // MOBA_BLOCK
    blk_ids = jnp.arange(n_blk, dtype=jnp.int32)
    gate = q @ kb.mean(axis=1).T
    gate = jnp.where(blk_ids[None, :] < own[:, None], gate, -jnp.inf)
    n_sel = min(MOBA_TOPK, n_blk)
    _, top = lax.top_k(gate, n_sel)
    valid = top < own[:, None]

    n_pairs = s * n_sel
    pair_blk = jnp.where(valid, top, n_blk).reshape(-1).astype(jnp.int32)
    pair_q = jnp.broadcast_to(t[:, None], (s, n_sel)).reshape(-1)
    order = jnp.argsort(pair_blk, stable=True).astype(jnp.int32)
    sorted_blk = pair_blk[order]
    counts = jax.ops.segment_sum(jnp.ones((n_pairs,), jnp.int32), pair_blk,
                                 num_segments=n_blk + 1)[:n_blk]
    first = jnp.cumsum(counts) - counts
    padded = (counts + MOBA_QG - 1) // MOBA_QG * MOBA_QG
    slot0 = jnp.cumsum(padded) - padded
    n_groups = -(-n_pairs // MOBA_QG) + n_blk
    n_slots = n_groups * MOBA_QG
    blk_c = jnp.minimum(sorted_blk, n_blk - 1)
    rank = jnp.arange(n_pairs, dtype=jnp.int32) - first[blk_c]
    slot = jnp.where(sorted_blk < n_blk, slot0[blk_c] + rank, n_slots)
    slot_pair = jnp.full((n_slots,), n_pairs, jnp.int32).at[slot].set(order, mode='drop')
    pair_c = jnp.minimum(slot_pair, n_pairs - 1)
    slot_q = pair_q[pair_c].reshape(n_groups, MOBA_QG)
    group_blk = jnp.minimum(pair_blk[pair_c], n_blk - 1).reshape(n_groups, MOBA_QG)[:, 0]

    qg = q[slot_q]
    kg, vg = kb[group_blk], vb[group_blk]
    dist = slot_q[:, :, None] - (group_blk[:, None, None] * MOBA_BLOCK + offs)
    lg = jnp.einsum('gqd,gkd->gqk', qg, kg) * scale - slope * dist.astype(f32)
    m_g = jnp.max(lg, axis=-1)
    p_g = jnp.exp(lg - m_g[..., None])
    l_g = jnp.sum(p_g, axis=-1)
    acc_g = jnp.einsum('gqk,gkd->gqd', p_g, vg)

    m_sel = jnp.full((n_pairs,), -jnp.inf, f32).at[slot_pair].set(m_g.reshape(-1), mode='drop')
    l_sel = jnp.zeros((n_pairs,), f32).at[slot_pair].set(l_g.reshape(-1), mode='drop')
    acc_sel = jnp.zeros((n_pairs, dh), f32).at[slot_pair].set(acc_g.reshape(-1, dh), mode='drop')

    m_all = jnp.concatenate([m_sel.reshape(s, n_sel), m_own[:, None]], axis=1)
    l_all = jnp.concatenate([l_sel.reshape(s, n_sel), l_own[:, None]], axis=1)
    acc_all = jnp.concatenate([acc_sel.reshape(s, n_sel, dh), acc_own[:, None]], axis=1)
    wts = jnp.exp(m_all - jnp.max(m_all, axis=1, keepdims=True))
    return jnp.einsum('sj,sjd->sd', wts, acc_all) / jnp.sum(wts * l_all, axis=1, keepdims=True)


def moba_attention(q, k, v, slopes):
    per_head = jax.vmap(moba_head, in_axes=(0, 0, 0, 0))
    return jax.vmap(per_head, in_axes=(0, 0, 0, None))(q, k, v, slopes)


def stick_breaking_attention(q, k, v):
    f32 = jnp.float32
    b, h, s, dh = q.shape
    q, k, v = q.astype(f32), k.astype(f32), v.astype(f32)
    scale = dh ** -0.5
    qbs = SB_Q_BLOCK
    incl = (jnp.arange(qbs)[:, None] >= jnp.arange(qbs)[None, :]).astype(f32)
    outs = []
    for start in range(0, s, SB_SPAN):
        end = min(start + SB_SPAN, s)
        n_q, n_t = (end - start) // qbs, end // qbs
        k_s, v_s = k[:, :, :end], v[:, :, :end]
        key_pos = jnp.arange(end, dtype=jnp.int32)
        q_blocks = q[:, :, start:end].reshape(b, h, n_q, qbs, dh).transpose(2, 0, 1, 3, 4)
        starts = start + jnp.arange(n_q, dtype=jnp.int32) * qbs

        def block(args):
            q_b, t0 = args
            z = jnp.einsum('bhqd,bhkd->bhqk', q_b, k_s) * scale
            t = t0 + jnp.arange(qbs, dtype=jnp.int32)
            past = key_pos[None, :] < t[:, None]
            lk = jnp.where(past, jax.nn.log_sigmoid(-z), 0.0)
            c = jnp.einsum('bhqnj,js->bhqns', lk.reshape(b, h, qbs, n_t, qbs), incl)
            tot = c[..., 0]
            later = lax.cumsum(tot, axis=3, reverse=True) - tot
            log_a = z + (c + later[..., None]).reshape(b, h, qbs, end)
            wts = jnp.where(past, jnp.exp(log_a), 0.0)
            return jnp.einsum('bhqk,bhkd->bhqd', wts, v_s)

        out = lax.map(block, (q_blocks, starts))
        outs.append(out.transpose(1, 2, 0, 3, 4).reshape(b, h, end - start, dh))
    return jnp.concatenate(outs, axis=2)


def rwkv7_chunked(r, log_w, k, v, a, bb):
    f32 = jnp.float32
    b, s, h, n = r.shape
    L = RWKV_CHUNK
    nc = s // L
    rs = lambda x: x.astype(f32).reshape(b, nc, L, h, n).transpose(0, 1, 3, 2, 4)
    r, log_w, k, v, a, bb = (rs(x) for x in (r, log_w, k, v, a, bb))
    lg = jnp.cumsum(log_w, axis=3)
    lg_last = lg[..., -1:, :]
    a_t = a * jnp.exp(lg - log_w)
    r_t = r * jnp.exp(lg)
    b_i = bb * jnp.exp(-lg)
    k_i = k * jnp.exp(-lg)
    b_hat = bb * jnp.exp(lg_last - lg)
    k_hat = k * jnp.exp(lg_last - lg)
    strict = jnp.tril(jnp.ones((L, L), bool), -1)
    inclusive = jnp.tril(jnp.ones((L, L), bool))
    pair = lambda x, y, m: jnp.where(m, jnp.einsum('bchtn,bchin->bchti', x, y), 0.0)
    m_ab, m_ak = pair(a_t, b_i, strict), pair(a_t, k_i, strict)
    m_rb, m_rk = pair(r_t, b_i, inclusive), pair(r_t, k_i, inclusive)
    eye = jnp.eye(L, dtype=f32)
    t_inv, pw = eye + m_ab, m_ab
    for _ in range(RWKV_CHUNK_LOG2 - 1):
        pw = pw @ pw
        t_inv = t_inv @ (eye + pw)
    ta = t_inv @ a_t
    w0 = t_inv @ (m_ak @ v)
    p_mat = jnp.einsum('bchid,bchie->bchde', ta, b_hat) + jnp.exp(lg_last[..., 0, :])[..., :, None] * jnp.eye(n, dtype=f32)
    q_mat = jnp.einsum('bchiv,bchie->bchve', w0, b_hat) + jnp.einsum('bchiv,bchie->bchve', v, k_hat)

    def step(state, pq):
        p_c, q_c = pq
        return jnp.einsum('bhvk,bhke->bhve', state, p_c) + q_c, state

    _, s0 = lax.scan(step, jnp.zeros((b, h, n, n), f32),
                     (jnp.moveaxis(p_mat, 1, 0), jnp.moveaxis(q_mat, 1, 0)))
    s0 = jnp.moveaxis(s0, 0, 1)
    u = jnp.einsum('bchtk,bchvk->bchtv', ta, s0) + w0
    y = jnp.einsum('bchtk,bchvk->bchtv', r_t, s0) + m_rb @ u + m_rk @ v
    return y.transpose(0, 1, 3, 2, 4).reshape(b, s, h, n)


def _linear_recurrence_combine(left, right):
    a_l, b_l = left
    a_r, b_r = right
    return a_l * a_r, a_r * b_l + b_r


def even_mixer(h, v_first, w_in, w_out):
    proj = h @ w_in
    wa, wb = N_HEADS_MOBA * HEAD_DIM, N_HEADS_SB * SB_HEAD_DIM
    qa, ka, va, qb, kb, vb = jnp.split(proj, [wa, 2 * wa, 3 * wa, 3 * wa + wb, 3 * wa + 2 * wb], axis=-1)
    oa = moba_attention(split_heads(qa, N_HEADS_MOBA), split_heads(ka, N_HEADS_MOBA),
                        split_heads(va, N_HEADS_MOBA), alibi_slopes(N_HEADS_MOBA))
    ob = stick_breaking_attention(split_heads(qb, N_HEADS_SB), split_heads(kb, N_HEADS_SB),
                                  split_heads(vb, N_HEADS_SB))
    out = jnp.concatenate([merge_heads(oa), merge_heads(ob)], axis=-1).astype(h.dtype) @ w_out
    return out, v_first


def odd_mixer(h, v_first, w_in, w_out, shift_mu, w0, w2, a0, a2, g2, k_k, k_a, r_k, lnx_w, lnx_b,
              conv_w, conv_b, gate_w, gate_b, lru_lambda, v0=None, v2=None):
    f32 = jnp.float32
    b, s, _ = h.shape
    proj = h @ w_in
    n_c = shift_mu.shape[0]
    pc, pd = proj[..., :n_c], proj[..., n_c:]
    pc = (pc + (token_shift(pc) - pc) * shift_mu).astype(f32)
    sizes = [RWKV_WIDTH] * 3 + [DECAY_LORA, AAA_LORA] + ([MV_LORA] if v0 is not None else []) + [GATE_LORA]
    parts = jnp.split(pc, np.cumsum(sizes)[:-1].tolist(), axis=-1)
    r, k, v, w_lo, a_lo = parts[:5]
    g_lo = parts[-1]
    w_log = -jax.nn.softplus(-(w0 + jnp.tanh(w_lo) @ w2)) - 0.5
    log_w = -jnp.exp(w_log)
    if v_first is None:
        v_first = v
    else:
        v = v + (v_first - v) * jax.nn.sigmoid(v0 + parts[5] @ v2)
    a = jax.nn.sigmoid(a0 + a_lo @ a2)
    g = jax.nn.sigmoid(g_lo) @ g2
    heads = lambda t: t.reshape(b, s, RWKV_HEADS, RWKV_HEAD)
    kk = heads(k * k_k)
    kk = kk / jnp.maximum(jnp.linalg.norm(kk, axis=-1, keepdims=True), 1e-12)
    k = k * (1.0 + (a - 1.0) * k_a)
    rh, kh, vh, ah = heads(r), heads(k), heads(v), heads(a)
    y = rwkv7_chunked(rh, heads(log_w), kh, vh, -kk, kk * ah)
    mu = jnp.mean(y, axis=-1, keepdims=True)
    var = jnp.mean(jnp.square(y - mu), axis=-1, keepdims=True)
    y = ((y - mu) * lax.rsqrt(var + GN_EPS)).reshape(b, s, RWKV_WIDTH) * lnx_w + lnx_b
    bonus = jnp.sum(rh * kh * r_k, axis=-1, keepdims=True) * vh
    c_out = (y + bonus.reshape(b, s, RWKV_WIDTH)) * g

    lru_gate, lru_x = jnp.split(pd.astype(f32), 2, axis=-1)
    xb = causal_dwconv(lru_x, conv_w) + conv_b
    xh = xb.reshape(b, s, LRU_BLOCKS, LRU_WIDTH // LRU_BLOCKS)
    gates = jnp.einsum('bsnc,gncd->gbsnd', xh, gate_w).reshape(2, b, s, LRU_WIDTH) + gate_b[:, None, None, :]
    log_a = -LRU_C * jax.nn.sigmoid(gates[0]) * jax.nn.softplus(-lru_lambda)
    a_t = jnp.exp(log_a)
    b_t = jnp.sqrt(-jnp.expm1(2.0 * log_a)) * jax.nn.sigmoid(gates[1]) * xb
    _, hs = lax.associative_scan(_linear_recurrence_combine, (a_t, b_t), axis=1)
    d_out = hs * jax.nn.gelu(lru_gate, approximate=True)

    out = jnp.concatenate([c_out, d_out], axis=-1).astype(h.dtype) @ w_out
    return out, v_first


def conv_ffn(h, w_up, conv_w, w_down):
    u = causal_dwconv(h @ w_up, conv_w)
    gate, val = jnp.split(u, 2, axis=-1)
    return (jax.nn.gelu(gate, approximate=True) * val) @ w_down


def setup_inputs(seed: int = 0) -> dict:
    key = jax.random.key(seed)
    keys = iter(jax.random.split(key, 8 + 24 * DEPTH))
    f32 = jnp.float32

    def normal(shape, scale):
        return scale * jax.random.normal(next(keys), shape, f32)

    def uniform(shape, lo, hi):
        return jax.random.uniform(next(keys), shape, f32, lo, hi)

    inp = {'x': normal((BATCH, SEQ, D_MODEL), 1.0)}
    for layer in range(DEPTH):
        p = 'l%d_' % layer
        inp[p + 'norms'] = 1.0 + normal((4, D_MODEL), 0.05)
        if layer % 2 == 0:
            attn_w = N_HEADS_MOBA * HEAD_DIM + N_HEADS_SB * SB_HEAD_DIM
            inp[p + 'w_in'] = normal((D_MODEL, 3 * attn_w), D_MODEL ** -0.5)
            inp[p + 'w_out'] = normal((attn_w, D_MODEL), attn_w ** -0.5)
        else:
            vres = layer != FIRST_RWKV_LAYER
            n_c = 3 * RWKV_WIDTH + DECAY_LORA + AAA_LORA + (MV_LORA if vres else 0) + GATE_LORA
            mix_w = RWKV_WIDTH + LRU_WIDTH
            inp[p + 'w_in'] = normal((D_MODEL, n_c + 2 * LRU_WIDTH), D_MODEL ** -0.5)
            inp[p + 'w_out'] = normal((mix_w, D_MODEL), mix_w ** -0.5)
            inp[p + 'shift_mu'] = uniform((n_c,), 0.0, 1.0)
            inp[p + 'w0'] = jnp.linspace(-6.0, -1.0, RWKV_WIDTH, dtype=f32) + normal((RWKV_WIDTH,), 0.1)
            inp[p + 'w2'] = normal((DECAY_LORA, RWKV_WIDTH), 0.5 * DECAY_LORA ** -0.5)
            inp[p + 'a0'] = normal((RWKV_WIDTH,), 0.5)
            inp[p + 'a2'] = normal((AAA_LORA, RWKV_WIDTH), AAA_LORA ** -0.5)
            if vres:
                inp[p + 'v0'] = normal((RWKV_WIDTH,), 0.5)
                inp[p + 'v2'] = normal((MV_LORA, RWKV_WIDTH), MV_LORA ** -0.5)
            inp[p + 'g2'] = normal((GATE_LORA, RWKV_WIDTH), GATE_LORA ** -0.5)
            inp[p + 'k_k'] = 0.85 + normal((RWKV_WIDTH,), 0.05)
            inp[p + 'k_a'] = 1.0 + normal((RWKV_WIDTH,), 0.05)
            inp[p + 'r_k'] = normal((RWKV_HEADS, RWKV_HEAD), 0.1)
            inp[p + 'lnx_w'] = 1.0 + normal((RWKV_WIDTH,), 0.05)
            inp[p + 'lnx_b'] = normal((RWKV_WIDTH,), 0.02)
            inp[p + 'conv_w'] = normal((LRU_CONV, LRU_WIDTH), LRU_CONV ** -0.5)
            inp[p + 'conv_b'] = normal((LRU_WIDTH,), 0.02)
            bw = LRU_WIDTH // LRU_BLOCKS
            inp[p + 'gate_w'] = normal((2, LRU_BLOCKS, bw, bw), bw ** -0.5)
            inp[p + 'gate_b'] = normal((2, LRU_WIDTH), 0.02)
            a_c = uniform((LRU_WIDTH,), 0.9, 0.999)
            a_base = a_c ** (1.0 / LRU_C)
            inp[p + 'lru_lambda'] = jnp.log(a_base) - jnp.log1p(-a_base)
        inp[p + 'ffn_up'] = normal((D_MODEL, 2 * D_FF), D_MODEL ** -0.5)
        inp[p + 'ffn_conv'] = normal((FFN_CONV, 2 * D_FF), FFN_CONV ** -0.5)
        inp[p + 'ffn_down'] = normal((D_FF, D_MODEL), D_FF ** -0.5)
    return inp


def reference(x,
              l0_norms, l0_w_in, l0_w_out, l0_ffn_up, l0_ffn_conv, l0_ffn_down,
              l1_norms, l1_w_in, l1_w_out, l1_shift_mu, l1_w0, l1_w2, l1_a0, l1_a2, l1_g2,
              l1_k_k, l1_k_a, l1_r_k, l1_lnx_w, l1_lnx_b, l1_conv_w, l1_conv_b, l1_gate_w, l1_gate_b,
              l1_lru_lambda, l1_ffn_up, l1_ffn_conv, l1_ffn_down,
              l2_norms, l2_w_in, l2_w_out, l2_ffn_up, l2_ffn_conv, l2_ffn_down,
              l3_norms, l3_w_in, l3_w_out, l3_shift_mu, l3_w0, l3_w2, l3_a0, l3_a2, l3_v0, l3_v2, l3_g2,
              l3_k_k, l3_k_a, l3_r_k, l3_lnx_w, l3_lnx_b, l3_conv_w, l3_conv_b, l3_gate_w, l3_gate_b,
              l3_lru_lambda, l3_ffn_up, l3_ffn_conv, l3_ffn_down):
    norms = [l0_norms, l1_norms, l2_norms, l3_norms]
    mixers = [
        functools.partial(even_mixer, w_in=l0_w_in, w_out=l0_w_out),
        functools.partial(odd_mixer, w_in=l1_w_in, w_out=l1_w_out, shift_mu=l1_shift_mu, w0=l1_w0,
                          w2=l1_w2, a0=l1_a0, a2=l1_a2, g2=l1_g2, k_k=l1_k_k, k_a=l1_k_a, r_k=l1_r_k,
                          lnx_w=l1_lnx_w, lnx_b=l1_lnx_b, conv_w=l1_conv_w, conv_b=l1_conv_b,
                          gate_w=l1_gate_w, gate_b=l1_gate_b, lru_lambda=l1_lru_lambda),
        functools.partial(even_mixer, w_in=l2_w_in, w_out=l2_w_out),
        functools.partial(odd_mixer, w_in=l3_w_in, w_out=l3_w_out, shift_mu=l3_shift_mu, w0=l3_w0,
                          w2=l3_w2, a0=l3_a0, a2=l3_a2, g2=l3_g2, k_k=l3_k_k, k_a=l3_k_a, r_k=l3_r_k,
                          lnx_w=l3_lnx_w, lnx_b=l3_lnx_b, conv_w=l3_conv_w, conv_b=l3_conv_b,
                          gate_w=l3_gate_w, gate_b=l3_gate_b, lru_lambda=l3_lru_lambda,
                          v0=l3_v0, v2=l3_v2),
    ]
    ffns = [(l0_ffn_up, l0_ffn_conv, l0_ffn_down), (l1_ffn_up, l1_ffn_conv, l1_ffn_down),
            (l2_ffn_up, l2_ffn_conv, l2_ffn_down), (l3_ffn_up, l3_ffn_conv, l3_ffn_down)]
    v_first = None
    for layer in range(DEPTH):
        g = norms[layer]
        mix, v_first = mixers[layer](rms_norm(x, g[0]), v_first)
        x = x + rms_norm(mix, g[1])
        w_up, w_conv, w_down = ffns[layer]
        x = x + rms_norm(conv_ffn(rms_norm(x, g[2]), w_up, w_conv, w_down), g[3])
    return x
```

```python
import functools
import math

import jax
import jax.numpy as jnp
from jax import lax
from jax.experimental import pallas as pl
from jax.experimental.pallas import tpu as pltpu

F32 = jnp.float32
BF16 = jnp.bfloat16

V7X_VMEM_BYTES = 64 * 1024 * 1024
VMEM_LIMIT = 56 * 1024 * 1024
LANES = 128
SUBLANES = 8

RMS_EPS = 1e-6
GN_EPS = 64e-5
MOBA_HEAD_DIM = 128
MOBA_BLOCK = 256
MOBA_TOPK = 3
SB_HEAD_DIM = 256
RWKV_HEAD = 64
LRU_BLOCKS = 16
LRU_CONV = 4
LRU_C = 8.0
FFN_CONV = 3
NEG_BIG = -1e30


def _params(*sem):
    return pltpu.CompilerParams(dimension_semantics=sem, vmem_limit_bytes=VMEM_LIMIT)


def _rms(x, g):
    return x * lax.rsqrt(jnp.mean(x * x, axis=-1, keepdims=True) + RMS_EPS) * g


def _rmsnorm_kernel(x_ref, g_ref, o_ref):
    o_ref[...] = _rms(x_ref[...], g_ref[...]).astype(o_ref.dtype)


def rmsnorm(x, g, tm=256):
    s, d = x.shape
    return pl.pallas_call(
        _rmsnorm_kernel,
        grid=(s // tm,),
        in_specs=[pl.BlockSpec((tm, d), lambda i: (i, 0)),
                  pl.BlockSpec((1, d), lambda i: (0, 0))],
        out_specs=pl.BlockSpec((tm, d), lambda i: (i, 0)),
        out_shape=jax.ShapeDtypeStruct((s, d), BF16),
        compiler_params=_params("parallel"),
        name="rmsnorm",
    )(x, g.reshape(1, d))


def _add_rmsnorm_kernel(x_ref, y_ref, g1_ref, g2_ref, xo_ref, ho_ref):
    xn = x_ref[...] + _rms(y_ref[...], g1_ref[...])
    xo_ref[...] = xn
    ho_ref[...] = _rms(xn, g2_ref[...]).astype(ho_ref.dtype)


def add_rmsnorm(x, y, g_post, g_pre, tm=256):
    s, d = x.shape
    row = pl.BlockSpec((tm, d), lambda i: (i, 0))
    vec = pl.BlockSpec((1, d), lambda i: (0, 0))
    return pl.pallas_call(
        _add_rmsnorm_kernel,
        grid=(s // tm,),
        in_specs=[row, row, vec, vec],
        out_specs=[row, row],
        out_shape=[jax.ShapeDtypeStruct((s, d), F32), jax.ShapeDtypeStruct((s, d), BF16)],
        compiler_params=_params("parallel"),
        name="add_rmsnorm",
    )(x, y, g_post.reshape(1, d), g_pre.reshape(1, d))


def _mm_kernel(a_ref, b_ref, o_ref):
    o_ref[...] = jnp.dot(a_ref[...], b_ref[...], preferred_element_type=F32).astype(o_ref.dtype)


def matmul(a, b, out_dtype, tm=512, tn=512):
    m, k = a.shape
    _, n = b.shape
    return pl.pallas_call(
        _mm_kernel,
        grid=(m // tm, n // tn),
        in_specs=[pl.BlockSpec((tm, k), lambda i, j: (i, 0)),
                  pl.BlockSpec((k, tn), lambda i, j: (0, j))],
        out_specs=pl.BlockSpec((tm, tn), lambda i, j: (i, j)),
        out_shape=jax.ShapeDtypeStruct((m, n), out_dtype),
        compiler_params=_params("parallel", "parallel"),
        name="matmul",
    )(a, b)


def _mm2_kernel(a1_ref, a2_ref, b1_ref, b2_ref, o_ref):
    acc = jnp.dot(a1_ref[...], b1_ref[...], preferred_element_type=F32)
    acc += jnp.dot(a2_ref[...], b2_ref[...], preferred_element_type=F32)
    o_ref[...] = acc.astype(o_ref.dtype)


def matmul_cat(a1, a2, b, out_dtype, tm=512, tn=512):
    m, k1 = a1.shape
    _, k2 = a2.shape
    assert k1 == k2 and b.shape[0] == k1 + k2
    n = b.shape[1]
    return pl.pallas_call(
        _mm2_kernel,
        grid=(m // tm, n // tn),
        in_specs=[pl.BlockSpec((tm, k1), lambda i, j: (i, 0)),
                  pl.BlockSpec((tm, k2), lambda i, j: (i, 0)),
                  pl.BlockSpec((k1, tn), lambda i, j: (0, j)),
                  pl.BlockSpec((k2, tn), lambda i, j: (1, j))],
        out_specs=pl.BlockSpec((tm, tn), lambda i, j: (i, j)),
        out_shape=jax.ShapeDtypeStruct((m, n), out_dtype),
        compiler_params=_params("parallel", "parallel"),
        name="matmul_cat",
    )(a1, a2, b, b)


def _shift_rows(x, prev, d):
    y = pltpu.roll(x, d, 0)
    row = lax.broadcasted_iota(jnp.int32, x.shape, 0)
    for r in range(d):
        y = jnp.where(row == r, prev[SUBLANES - d + r:SUBLANES - d + r + 1, :], y)
    return y


def _ffn_up_kernel(h_ref, wg_ref, wv_ref, cg_ref, cv_ref, o_ref, hg_ref, hv_ref):
    i = pl.program_id(1)

    @pl.when(i == 0)
    def _():
        hg_ref[...] = jnp.zeros_like(hg_ref)
        hv_ref[...] = jnp.zeros_like(hv_ref)

    h = h_ref[...]

    def conv(w_ref, c_ref, halo_ref):
        raw = jnp.dot(h, w_ref[...], preferred_element_type=F32)
        prev = halo_ref[...]
        c = c_ref[...]
        out = (c[2:3, :] * raw + c[1:2, :] * _shift_rows(raw, prev, 1)
               + c[0:1, :] * _shift_rows(raw, prev, 2))
        halo_ref[...] = raw[raw.shape[0] - SUBLANES:, :]
        return out

    gate = conv(wg_ref, cg_ref, hg_ref)
    val = conv(wv_ref, cv_ref, hv_ref)
    o_ref[...] = (jax.nn.gelu(gate, approximate=True) * val).astype(o_ref.dtype)


def ffn_up(h, w_up, conv_w, tm=512, tn=512):
    s, d = h.shape
    f = w_up.shape[1] // 2
    nj = f // tn
    return pl.pallas_call(
        _ffn_up_kernel,
        grid=(nj, s // tm),
        in_specs=[pl.BlockSpec((tm, d), lambda j, i: (i, 0)),
                  pl.BlockSpec((d, tn), lambda j, i: (0, j)),
                  pl.BlockSpec((d, tn), lambda j, i: (0, j + nj)),
                  pl.BlockSpec((FFN_CONV, tn), lambda j, i: (0, j)),
                  pl.BlockSpec((FFN_CONV, tn), lambda j, i: (0, j + nj))],
        out_specs=pl.BlockSpec((tm, tn), lambda j, i: (i, j)),
        out_shape=jax.ShapeDtypeStruct((s, f), BF16),
        scratch_shapes=[pltpu.VMEM((SUBLANES, tn), F32), pltpu.VMEM((SUBLANES, tn), F32)],
        compiler_params=_params("parallel", "arbitrary"),
        name="ffn_up",
    )(h, w_up, w_up, conv_w, conv_w)


def _dot_nt(a, b):
    return lax.dot_general(a, b, (((1,), (1,)), ((), ())), preferred_element_type=F32)


def _moba_kernel(slope_ref, q_ref, k_ref, v_ref, o_ref, kbar_ref, *, n_blk):
    hd = pl.program_id(0)
    i = pl.program_id(1)
    bs = MOBA_BLOCK
    scale = MOBA_HEAD_DIM ** -0.5
    slope = slope_ref[hd]

    @pl.when(i == 0)
    def _():
        def body(j, c):
            kb = k_ref[pl.ds(pl.multiple_of(j * bs, bs), bs), :].astype(F32)
            kbar_ref[pl.ds(j, 1), :] = jnp.mean(kb, axis=0, keepdims=True)
            return c
        lax.fori_loop(0, n_blk, body, 0)

    q = q_ref[...]
    kbar = kbar_ref[...]
    kb_hi = kbar.astype(BF16)
    kb_lo = (kbar - kb_hi.astype(F32)).astype(BF16)
    gate = _dot_nt(q, kb_hi) + _dot_nt(q, kb_lo)
    blk = lax.broadcasted_iota(jnp.int32, gate.shape, 1)
    gate = jnp.where(blk < i, gate, -jnp.inf)
    sel = []
    for _ in range(MOBA_TOPK):
        m = jnp.max(gate, axis=-1, keepdims=True)
        cand = (gate == m) & (gate > -jnp.inf)
        idx = jnp.min(jnp.where(cand, blk, n_blk), axis=-1, keepdims=True)
        sel.append(idx)
        gate = jnp.where(blk == idx, -jnp.inf, gate)

    col = lax.broadcasted_iota(jnp.int32, (bs, bs), 1)
    row = lax.broadcasted_iota(jnp.int32, (bs, bs), 0)
    colf = col.astype(F32)

    i0 = pl.multiple_of(i * bs, bs)
    k_own = k_ref[pl.ds(i0, bs), :]
    v_own = v_ref[pl.ds(i0, bs), :]
    s = _dot_nt(q, k_own) * scale + slope * colf
    s = jnp.where(col <= row, s, NEG_BIG)
    m0 = jnp.max(s, axis=-1, keepdims=True)
    p = jnp.exp(s - m0)
    l0 = jnp.sum(p, axis=-1, keepdims=True)
    acc0 = jnp.dot(p.astype(BF16), v_own, preferred_element_type=F32)

    def body(j, carry):
        m_i, l_i, acc = carry
        j0 = pl.multiple_of(j * bs, bs)
        kj = k_ref[pl.ds(j0, bs), :]
        vj = v_ref[pl.ds(j0, bs), :]
        picked = (sel[0] == j) | (sel[1] == j) | (sel[2] == j)
        qbias = jnp.where(picked, 0.0, NEG_BIG)
        cbias = slope * (colf + ((j - i) * bs).astype(F32))
        s = _dot_nt(q, kj) * scale + cbias
        m_new = jnp.maximum(m_i, jnp.max(s, axis=-1, keepdims=True) + qbias)
        p = jnp.exp(s + (qbias - m_new))
        alpha = jnp.exp(m_i - m_new)
        l_new = alpha * l_i + jnp.sum(p, axis=-1, keepdims=True)
        acc_new = alpha * acc + jnp.dot(p.astype(BF16), vj, preferred_element_type=F32)
        return m_new, l_new, acc_new

    _, l_f, acc_f = lax.fori_loop(0, i, body, (m0, l0, acc0))
    o_ref[...] = (acc_f / l_f).astype(o_ref.dtype)


def moba_attention(proj, n_heads):
    s = proj.shape[0]
    dh, bs = MOBA_HEAD_DIM, MOBA_BLOCK
    n_blk = s // bs
    slopes = 2.0 ** (-8.0 * jnp.arange(1, n_heads + 1, dtype=F32) / n_heads)
    grid_spec = pltpu.PrefetchScalarGridSpec(
        num_scalar_prefetch=1,
        grid=(n_heads, n_blk),
        in_specs=[pl.BlockSpec((bs, dh), lambda h, i, sl: (i, h)),
                  pl.BlockSpec((s, dh), lambda h, i, sl: (0, n_heads + h)),
                  pl.BlockSpec((s, dh), lambda h, i, sl: (0, 2 * n_heads + h))],
        out_specs=pl.BlockSpec((bs, dh), lambda h, i, sl: (i, h)),
        scratch_shapes=[pltpu.VMEM((n_blk, dh), F32)],
    )
    return pl.pallas_call(
        functools.partial(_moba_kernel, n_blk=n_blk),
        grid_spec=grid_spec,
        out_shape=jax.ShapeDtypeStruct((s, n_heads * dh), BF16),
        compiler_params=_params("parallel", "arbitrary"),
        name="moba",
    )(slopes, proj, proj, proj)


def _softplus(z):
    return jnp.maximum(z, 0.0) + jnp.log1p(jnp.exp(-jnp.abs(z)))


def _sb_kernel(q_ref, k_ref, v_ref, o_ref, *, tb):
    i = pl.program_id(1)
    scale = SB_HEAD_DIM ** -0.5
    q = q_ref[...]
    row = lax.broadcasted_iota(jnp.int32, (tb, tb), 0)
    col = lax.broadcasted_iota(jnp.int32, (tb, tb), 1)
    tri = jnp.where(row >= col, 1.0, 0.0).astype(BF16)
    past = col < row

    def tile(kj, vj, later, masked):
        z = _dot_nt(q, kj) * scale
        lk = -_softplus(z)
        if masked:
            lk = jnp.where(past, lk, 0.0)
        c = jnp.dot(lk.astype(BF16), tri, preferred_element_type=F32)
        w = jnp.exp(z + c + later)
        if masked:
            w = jnp.where(past, w, 0.0)
        out = jnp.dot(w.astype(BF16), vj, preferred_element_type=F32)
        return out, later + jnp.sum(lk, axis=-1, keepdims=True)

    i0 = pl.multiple_of(i * tb, tb)
    acc0, later0 = tile(k_ref[pl.ds(i0, tb), :], v_ref[pl.ds(i0, tb), :],
                        jnp.zeros((tb, 1), F32), True)

    def body(t, carry):
        acc, later = carry
        j0 = pl.multiple_of((i - 1 - t) * tb, tb)
        out, later = tile(k_ref[pl.ds(j0, tb), :], v_ref[pl.ds(j0, tb), :], later, False)
        return acc + out, later

    acc, _ = lax.fori_loop(0, i, body, (acc0, later0))
    o_ref[...] = acc.astype(o_ref.dtype)


def sb_attention(proj, n_heads, col0, tb=256):
    s = proj.shape[0]
    dh = SB_HEAD_DIM
    c0 = col0 // dh
    return pl.pallas_call(
        functools.partial(_sb_kernel, tb=tb),
        grid=(n_heads, s // tb),
        in_specs=[pl.BlockSpec((tb, dh), lambda h, i: (i, c0 + h)),
                  pl.BlockSpec((s, dh), lambda h, i: (0, c0 + n_heads + h)),
                  pl.BlockSpec((s, dh), lambda h, i: (0, c0 + 2 * n_heads + h))],
        out_specs=pl.BlockSpec((tb, dh), lambda h, i: (i, h)),
        out_shape=jax.ShapeDtypeStruct((s, n_heads * dh), BF16),
        compiler_params=_params("parallel", "arbitrary"),
        name="stickbreak",
    )(proj, proj, proj)


RWKV_W = 2048
LORA_W = 640
HEAD_GROUP = 256
RWKV_CHUNK = 64


def _sigmoid(x):
    return 1.0 / (1.0 + jnp.exp(-x))


def _head_ones(n):
    r = lax.broadcasted_iota(jnp.int32, (n, n), 0) // RWKV_HEAD
    c = lax.broadcasted_iota(jnp.int32, (n, n), 1) // RWKV_HEAD
    return jnp.where(r == c, 1.0, 0.0).astype(BF16)


def _head_sum(x):
    e = _head_ones(HEAD_GROUP)
    hi = x.astype(BF16)
    lo = (x - hi.astype(F32)).astype(BF16)
    outs = []
    for c in range(x.shape[1] // HEAD_GROUP):
        sl = slice(c * HEAD_GROUP, (c + 1) * HEAD_GROUP)
        outs.append(jnp.dot(hi[:, sl], e, preferred_element_type=F32)
                    + jnp.dot(lo[:, sl], e, preferred_element_type=F32))
    return jnp.concatenate(outs, axis=-1)


def _token_shift(x_ref, halo_ref, mu_ref, first):
    x = x_ref[...]
    prev = jnp.where(first, 0.0, halo_ref[...])
    return x + (_shift_rows(x, prev, 1) - x) * mu_ref[...]


def _rwkv_prep_kernel(*refs, vres):
    (r_ref, k_ref, v_ref, lo_ref, rh_ref, kh_ref, vh_ref, loh_ref,
     mur_ref, muk_ref, muv_ref, mulo_ref,
     w0_ref, w2_ref, a0_ref, a2_ref, v0_ref, v2_ref, g2_ref, kk_ref, ka_ref, rk_ref) = refs[:22]
    rest = refs[22:]
    if vres:
        vf_ref, rest = rest[0], rest[1:]
    ro_ref, lw_ref, ko_ref, vo_ref, na_ref, bb_ref, g_ref, bo_ref = rest
    first = pl.program_id(0) == 0
    r = _token_shift(r_ref, rh_ref, mur_ref, first)
    k = _token_shift(k_ref, kh_ref, muk_ref, first)
    v = _token_shift(v_ref, vh_ref, muv_ref, first)
    lo = _token_shift(lo_ref, loh_ref, mulo_ref, first)
    w_lo, a_lo, v_lo, g_lo = lo[:, 0:128], lo[:, 128:256], lo[:, 256:384], lo[:, 384:640]

    def lora(x, w_ref):
        return jnp.dot(x.astype(BF16), w_ref[...], preferred_element_type=F32)

    w_log = -_softplus(-(w0_ref[...] + lora(jnp.tanh(w_lo), w2_ref))) - 0.5
    lw_ref[...] = -jnp.exp(w_log)
    if vres:
        v = v + (vf_ref[...] - v) * _sigmoid(v0_ref[...] + lora(v_lo, v2_ref))
    a = _sigmoid(a0_ref[...] + lora(a_lo, a2_ref))
    g_ref[...] = lora(_sigmoid(g_lo), g2_ref)
    kk = k * kk_ref[...]
    kk = kk / jnp.maximum(jnp.sqrt(_head_sum(kk * kk)), 1e-12)
    k2 = k * (1.0 + (a - 1.0) * ka_ref[...])
    ro_ref[...] = r
    ko_ref[...] = k2
    vo_ref[...] = v
    na_ref[...] = -kk
    bb_ref[...] = kk * a
    bo_ref[...] = _head_sum(r * k2 * rk_ref[...]) * v


def rwkv_prep(proj, col0, mu, w0, w2p, a0, a2p, v0, v2p, g2, k_k, k_a, r_k, v_first, tm=128):
    s = proj.shape[0]
    w = RWKV_W
    vres = v_first is not None
    c_r, c_lo = col0 // w, (col0 + 3 * w) // LORA_W
    hb = tm // SUBLANES

    def main(width, cidx):
        return pl.BlockSpec((tm, width), lambda i: (i, cidx))

    def halo(width, cidx):
        return pl.BlockSpec((SUBLANES, width), lambda i: (jnp.maximum(i * hb - 1, 0), cidx))

    def vec(width, cidx=0):
        return pl.BlockSpec((1, width), lambda i: (0, cidx))

    def full(shape):
        return pl.BlockSpec(shape, lambda i: (0, 0))

    mu2 = mu.reshape(1, -1)
    r1 = lambda t: t.reshape(1, -1)
    in_specs = [main(w, c_r), main(w, c_r + 1), main(w, c_r + 2), main(LORA_W, c_lo),
                halo(w, c_r), halo(w, c_r + 1), halo(w, c_r + 2), halo(LORA_W, c_lo),
                vec(w, c_r), vec(w, c_r + 1), vec(w, c_r + 2), vec(LORA_W, c_lo),
                vec(w), full((128, w)), vec(w), full((128, w)), vec(w), full((128, w)),
                full((256, w)), vec(w), vec(w), vec(w)]
    args = [proj] * 8 + [mu2] * 4 + [r1(w0), w2p, r1(a0), a2p, r1(v0), v2p, g2, r1(k_k), r1(k_a), r1(r_k)]
    if vres:
        in_specs.append(main(w, 0))
        args.append(v_first)
    out = pl.BlockSpec((tm, w), lambda i: (i, 0))
    return pl.pallas_call(
        functools.partial(_rwkv_prep_kernel, vres=vres),
        grid=(s // tm,),
        in_specs=in_specs,
        out_specs=[out] * 8,
        out_shape=[jax.ShapeDtypeStruct((s, w), F32)] * 8,
        compiler_params=_params("parallel"),
        name="rwkv_prep",
    )(*args)


def _cumsum_rows(x):
    n = x.shape[0]
    row = lax.broadcasted_iota(jnp.int32, x.shape, 0)
    d = 1
    while d < n:
        x = x + jnp.where(row >= d, pltpu.roll(x, d, 0), 0.0)
        d *= 2
    return x


def _rwkv_chunk(r, lw, k, v, a, b, state):
    L, n2 = r.shape
    bd = lambda x: x.astype(BF16)
    mm = lambda x, y: jnp.dot(bd(x), bd(y), preferred_element_type=F32)
    lg = _cumsum_rows(lw)
    tot = lg[L - 1:L, :]
    e_neg = jnp.exp(-lg)
    e_hat = jnp.exp(tot - lg)
    a_t = a * jnp.exp(lg - lw)
    r_t = r * jnp.exp(lg)
    b_i, k_i = b * e_neg, k * e_neg
    b_hat, k_hat = b * e_hat, k * e_hat

    lane = lax.broadcasted_iota(jnp.int32, (L, n2), 1)
    head_a = lane < RWKV_HEAD

    def stack(x):
        return jnp.concatenate([jnp.where(head_a, x, 0.0), jnp.where(head_a, 0.0, x)], axis=0)

    def fold(x):
        return x[:L] + x[L:]

    row = lax.broadcasted_iota(jnp.int32, (2 * L, 2 * L), 0)
    col = lax.broadcasted_iota(jnp.int32, (2 * L, 2 * L), 1)
    same = (row // L) == (col // L)
    strict = same & (col < row)
    incl = same & (col <= row)
    eye = jnp.where(row == col, 1.0, 0.0)

    x_a, x_r = stack(a_t), stack(r_t)
    ybk = jnp.concatenate([b_i, b_i, k_i, k_i], axis=0)
    pa = _dot_nt(bd(x_a), bd(ybk))
    pr = _dot_nt(bd(x_r), bd(ybk))
    m_ab = jnp.where(strict, pa[:, :2 * L], 0.0)
    m_ak = jnp.where(strict, pa[:, 2 * L:], 0.0)
    m_rb = jnp.where(incl, pr[:, :2 * L], 0.0)
    m_rk = jnp.where(incl, pr[:, 2 * L:], 0.0)

    t_inv, pw = eye + m_ab, m_ab
    for _ in range(int(math.log2(L)) - 1):
        pw = mm(pw, pw)
        t_inv = mm(t_inv, eye + pw)

    v_st = stack(v)
    w1 = mm(m_ak, v_st)
    tw = mm(t_inv, jnp.concatenate([x_a, w1], axis=1))
    ta, w0 = fold(tw[:, :n2]), fold(tw[:, n2:])
    gs = _dot_nt(bd(jnp.concatenate([ta, r_t], axis=0)), bd(state))
    u = gs[:L] + w0
    yj = mm(jnp.concatenate([m_rb, m_rk], axis=1), jnp.concatenate([stack(u), v_st], axis=0))
    y = gs[L:] + fold(yj)

    lhs_t = jnp.concatenate([jnp.concatenate([ta, jnp.zeros_like(ta)], axis=0).T,
                             jnp.concatenate([w0, v], axis=0).T], axis=0)
    pq = mm(lhs_t, jnp.concatenate([b_hat, k_hat], axis=0))
    p_mat = jnp.where(same, pq[:n2], 0.0) + eye * jnp.exp(tot)
    q_mat = jnp.where(same, pq[n2:], 0.0)
    return y, mm(state, p_mat) + q_mat


def _rwkv_rec_kernel(r_ref, lw_ref, k_ref, v_ref, a_ref, b_ref, y_ref, st_ref, *, n_pairs, n_chunks):
    @pl.when(pl.program_id(1) == 0)
    def _():
        st_ref[...] = jnp.zeros_like(st_ref)

    def body(c, carry):
        rows = pl.ds(pl.multiple_of(c * RWKV_CHUNK, RWKV_CHUNK), RWKV_CHUNK)
        for p in range(n_pairs):
            lanes = slice(p * LANES, (p + 1) * LANES)
            ld = lambda ref: ref[rows, lanes]
            y, st = _rwkv_chunk(ld(r_ref), ld(lw_ref), ld(k_ref), ld(v_ref), ld(a_ref), ld(b_ref), st_ref[p])
            y_ref[rows, lanes] = y
            st_ref[p] = st
        return carry

    lax.fori_loop(0, n_chunks, body, 0)


def rwkv_recurrence(r, lw, k, v, a, b, tt=512, n_pairs=2):
    s, w = r.shape
    wb = n_pairs * LANES
    blk = pl.BlockSpec((tt, wb), lambda h, t: (t, h))
    return pl.pallas_call(
        functools.partial(_rwkv_rec_kernel, n_pairs=n_pairs, n_chunks=tt // RWKV_CHUNK),
        grid=(w // wb, s // tt),
        in_specs=[blk] * 6,
        out_specs=blk,
        out_shape=jax.ShapeDtypeStruct((s, w), F32),
        scratch_shapes=[pltpu.VMEM((n_pairs, LANES, LANES), F32)],
        compiler_params=_params("arbitrary", "arbitrary"),
        name="rwkv_rec",
    )(r, lw, k, v, a, b)


def _rwkv_post_kernel(y_ref, g_ref, bo_ref, lw_ref, lb_ref, o_ref):
    y = y_ref[...]
    inv_n = 1.0 / RWKV_HEAD
    mu = _head_sum(y) * inv_n
    yc = y - mu
    var = _head_sum(yc * yc) * inv_n
    yn = yc * lax.rsqrt(var + GN_EPS) * lw_ref[...] + lb_ref[...]
    o_ref[...] = ((yn + bo_ref[...]) * g_ref[...]).astype(o_ref.dtype)


def rwkv_post(y, g, bonus, lnx_w, lnx_b, tm=256):
    s, w = y.shape
    row = pl.BlockSpec((tm, w), lambda i: (i, 0))
    vec = pl.BlockSpec((1, w), lambda i: (0, 0))
    return pl.pallas_call(
        _rwkv_post_kernel,
        grid=(s // tm,),
        in_specs=[row, row, row, vec, vec],
        out_specs=row,
        out_shape=jax.ShapeDtypeStruct((s, w), BF16),
        compiler_params=_params("parallel"),
        name="rwkv_post",
    )(y, g, bonus, lnx_w.reshape(1, w), lnx_b.reshape(1, w))


def _expm1(y):
    u = jnp.exp(y)
    near = (u - 1.0) * y / jnp.log(u)
    return jnp.where(u == 1.0, y, jnp.where(y < -0.5, u - 1.0, near))


def _lru_kernel(gate_ref, x_ref, xh_ref, cw_ref, cb_ref, gw_ref, gb_ref, lam_ref, o_ref, h_ref, *, nb):
    t = pl.program_id(1)

    @pl.when(t == 0)
    def _():
        h_ref[...] = jnp.zeros_like(h_ref)

    x = x_ref[...]
    prev = jnp.where(t == 0, 0.0, xh_ref[...])
    cw = cw_ref[...]
    xb = cw[3:4, :] * x + cb_ref[...]
    for d in range(1, LRU_CONV):
        xb = xb + cw[3 - d:4 - d, :] * _shift_rows(x, prev, d)
    xb16 = xb.astype(BF16)
    g0, g1 = [], []
    for n in range(nb):
        xs = xb16[:, n * LANES:(n + 1) * LANES]
        g0.append(jnp.dot(xs, gw_ref[0, n], preferred_element_type=F32))
        g1.append(jnp.dot(xs, gw_ref[1, n], preferred_element_type=F32))
    gb = gb_ref[...]
    g0 = jnp.concatenate(g0, axis=-1) + gb[0:1, :]
    g1 = jnp.concatenate(g1, axis=-1) + gb[1:2, :]
    log_a = -LRU_C * _sigmoid(g0) * _softplus(-lam_ref[...])
    a = jnp.exp(log_a)
    b = jnp.sqrt(-_expm1(2.0 * log_a)) * _sigmoid(g1) * xb
    n = x.shape[0]
    row = lax.broadcasted_iota(jnp.int32, x.shape, 0)
    d = 1
    while d < n:
        keep = row >= d
        a_s = jnp.where(keep, pltpu.roll(a, d, 0), 1.0)
        b_s = jnp.where(keep, pltpu.roll(b, d, 0), 0.0)
        b = a * b_s + b
        a = a * a_s
        d *= 2
    h = b + a * h_ref[SUBLANES - 1:SUBLANES, :]
    h_ref[...] = h[n - SUBLANES:, :]
    o_ref[...] = (h * jax.nn.gelu(gate_ref[...], approximate=True)).astype(o_ref.dtype)


def rg_lru(proj, conv_w, conv_b, gate_w, gate_b, lam, tm=256, tc=512):
    s = proj.shape[0]
    w = conv_w.shape[1]
    nj = w // tc
    nb = tc // LANES
    hb = tm // SUBLANES
    return pl.pallas_call(
        functools.partial(_lru_kernel, nb=nb),
        grid=(nj, s // tm),
        in_specs=[pl.BlockSpec((tm, tc), lambda j, t: (t, j)),
                  pl.BlockSpec((tm, tc), lambda j, t: (t, nj + j)),
                  pl.BlockSpec((SUBLANES, tc), lambda j, t: (jnp.maximum(t * hb - 1, 0), nj + j)),
                  pl.BlockSpec((LRU_CONV, tc), lambda j, t: (0, j)),
                  pl.BlockSpec((1, tc), lambda j, t: (0, j)),
                  pl.BlockSpec((2, nb, LANES, LANES), lambda j, t: (0, j, 0, 0)),
                  pl.BlockSpec((2, tc), lambda j, t: (0, j)),
                  pl.BlockSpec((1, tc), lambda j, t: (0, j))],
        out_specs=pl.BlockSpec((tm, tc), lambda j, t: (t, j)),
        out_shape=jax.ShapeDtypeStruct((s, w), BF16),
        scratch_shapes=[pltpu.VMEM((SUBLANES, tc), F32)],
        compiler_params=_params("arbitrary", "arbitrary"),
        name="rg_lru",
    )(proj, proj, proj, conv_w, conv_b.reshape(1, w), gate_w.astype(BF16), gate_b, lam.reshape(1, w))


def even_mixer(h, w_in, w_out):
    n_moba = (w_out.shape[0] // 2) // MOBA_HEAD_DIM
    n_sb = (w_out.shape[0] // 2) // SB_HEAD_DIM
    proj = matmul(h, w_in.astype(BF16), BF16)
    oa = moba_attention(proj, n_moba)
    ob = sb_attention(proj, n_sb, 3 * n_moba * MOBA_HEAD_DIM)
    return matmul_cat(oa, ob, w_out.astype(BF16), F32)


def _pad_cols(t, width):
    return jnp.pad(t, [(0, 0)] * (t.ndim - 1) + [(0, width - t.shape[-1])])


def _odd_layout(t, n_c, vres):
    w = RWKV_W
    sizes = [96, 96] + ([64] if vres else []) + [256]
    segs, c = [], 3 * w
    for sz in sizes:
        segs.append(_pad_cols(t[..., c:c + sz], LANES if sz < LANES else sz))
        c += sz
    if not vres:
        segs.insert(2, jnp.zeros(t.shape[:-1] + (LANES,), t.dtype))
    assert c == n_c
    return jnp.concatenate([t[..., n_c:], t[..., :3 * w]] + segs, axis=-1)


def _pad_rows(t, rows):
    return jnp.pad(t, ((0, rows - t.shape[0]), (0, 0)))


def odd_mixer(h, v_first, p):
    w = RWKV_W
    vres = "v2" in p
    n_c = p["shift_mu"].shape[0]
    w_in = _odd_layout(p["w_in"], n_c, vres).astype(BF16)
    lru_w = p["w_in"].shape[1] - n_c
    mu = _odd_layout(jnp.concatenate([p["shift_mu"], jnp.zeros((lru_w,), F32)]), n_c, vres)
    proj = matmul(h, w_in, F32, tn=LORA_W)
    zeros_w = jnp.zeros((w,), F32)
    r, lw, k, v, na, bb, g, bonus = rwkv_prep(
        proj, lru_w, mu, p["w0"], _pad_rows(p["w2"], LANES).astype(BF16),
        p["a0"], _pad_rows(p["a2"], LANES).astype(BF16),
        p["v0"] if vres else zeros_w,
        _pad_rows(p["v2"], LANES).astype(BF16) if vres else jnp.zeros((LANES, w), BF16),
        p["g2"].astype(BF16), p["k_k"], p["k_a"], p["r_k"].reshape(-1), v_first)
    if v_first is None:
        v_first = v
    y = rwkv_recurrence(r, lw, k, v, na, bb)
    c_out = rwkv_post(y, g, bonus, p["lnx_w"], p["lnx_b"])
    d_out = rg_lru(proj, p["conv_w"], p["conv_b"], p["gate_w"], p["gate_b"], p["lru_lambda"])
    return matmul_cat(c_out, d_out, p["w_out"].astype(BF16), F32), v_first


_ODD_NAMES = ("w_in", "w_out", "shift_mu", "w0", "w2", "a0", "a2", "g2", "k_k", "k_a", "r_k",
              "lnx_w", "lnx_b", "conv_w", "conv_b", "gate_w", "gate_b", "lru_lambda")


def _forward(x, layers):
    x2 = x.reshape(x.shape[-2], x.shape[-1])
    h = rmsnorm(x2, layers[0]["norms"][0])
    v_first = None
    for li, p in enumerate(layers):
        g = p["norms"]
        if "shift_mu" in p:
            mix, v_first = odd_mixer(h, v_first, p)
        else:
            mix = even_mixer(h, p["w_in"], p["w_out"])
        x2, h = add_rmsnorm(x2, mix, g[1], g[2])
        act = ffn_up(h, p["ffn_up"].astype(BF16), p["ffn_conv"])
        y = matmul(act, p["ffn_down"].astype(BF16), F32)
        g_next = layers[li + 1]["norms"][0] if li + 1 < len(layers) else g[3]
        x2, h = add_rmsnorm(x2, y, g[3], g_next)
    return x2.reshape(x.shape)


def kernel(x, l0_norms, l0_w_in, l0_w_out, l0_ffn_up, l0_ffn_conv, l0_ffn_down, l1_norms, l1_w_in, l1_w_out, l1_shift_mu, l1_w0, l1_w2, l1_a0, l1_a2, l1_g2, l1_k_k, l1_k_a, l1_r_k, l1_lnx_w, l1_lnx_b, l1_conv_w, l1_conv_b, l1_gate_w, l1_gate_b, l1_lru_lambda, l1_ffn_up, l1_ffn_conv, l1_ffn_down, l2_norms, l2_w_in, l2_w_out, l2_ffn_up, l2_ffn_conv, l2_ffn_down, l3_norms, l3_w_in, l3_w_out, l3_shift_mu, l3_w0, l3_w2, l3_a0, l3_a2, l3_v0, l3_v2, l3_g2, l3_k_k, l3_k_a, l3_r_k, l3_lnx_w, l3_lnx_b, l3_conv_w, l3_conv_b, l3_gate_w, l3_gate_b, l3_lru_lambda, l3_ffn_up, l3_ffn_conv, l3_ffn_down):
    a = locals()
    layers = []
    for li in range(4):
        pre = "l%d_" % li
        layers.append({k[len(pre):]: v for k, v in a.items() if k.startswith(pre)})
    return _forward(x, layers)
```

```python
import functools
import math

import jax
import jax.numpy as jnp
from jax import lax
from jax.experimental import pallas as pl
from jax.experimental.pallas import tpu as pltpu

F32 = jnp.float32
BF16 = jnp.bfloat16

V7X_VMEM_BYTES = 64 * 1024 * 1024
VMEM_LIMIT = 56 * 1024 * 1024
LANES = 128
SUBLANES = 8

RMS_EPS = 1e-6
GN_EPS = 64e-5
MOBA_HEAD_DIM = 128
MOBA_BLOCK = 256
MOBA_TOPK = 3
SB_HEAD_DIM = 256
RWKV_HEAD = 64
LRU_BLOCKS = 16
LRU_CONV = 4
LRU_C = 8.0
FFN_CONV = 3
NEG_BIG = -1e30
LOG2E = 1.4426950408889634
LN2 = 0.6931471805599453


def _params(*sem):
    return pltpu.CompilerParams(dimension_semantics=sem, vmem_limit_bytes=VMEM_LIMIT)


def _rms(x, g):
    return x * lax.rsqrt(jnp.mean(x * x, axis=-1, keepdims=True) + RMS_EPS) * g


def _rmsnorm_kernel(x_ref, g_ref, o_ref):
    o_ref[...] = _rms(x_ref[...], g_ref[...]).astype(o_ref.dtype)


def rmsnorm(x, g, tm=256):
    s, d = x.shape
    return pl.pallas_call(
        _rmsnorm_kernel,
        grid=(s // tm,),
        in_specs=[pl.BlockSpec((tm, d), lambda i: (i, 0)),
                  pl.BlockSpec((1, d), lambda i: (0, 0))],
        out_specs=pl.BlockSpec((tm, d), lambda i: (i, 0)),
        out_shape=jax.ShapeDtypeStruct((s, d), BF16),
        compiler_params=_params("parallel"),
        name="rmsnorm",
    )(x, g.reshape(1, d))


def _add_rmsnorm_kernel(x_ref, y_ref, g1_ref, g2_ref, xo_ref, ho_ref):
    xn = x_ref[...] + _rms(y_ref[...], g1_ref[...])
    xo_ref[...] = xn
    ho_ref[...] = _rms(xn, g2_ref[...]).astype(ho_ref.dtype)


def add_rmsnorm(x, y, g_post, g_pre, tm=256):
    s, d = x.shape
    row = pl.BlockSpec((tm, d), lambda i: (i, 0))
    vec = pl.BlockSpec((1, d), lambda i: (0, 0))
    return pl.pallas_call(
        _add_rmsnorm_kernel,
        grid=(s // tm,),
        in_specs=[row, row, vec, vec],
        out_specs=[row, row],
        out_shape=[jax.ShapeDtypeStruct((s, d), F32), jax.ShapeDtypeStruct((s, d), BF16)],
        compiler_params=_params("parallel"),
        name="add_rmsnorm",
    )(x, y, g_post.reshape(1, d), g_pre.reshape(1, d))


def _mm_kernel(a_ref, b_ref, o_ref):
    o_ref[...] = jnp.dot(a_ref[...], b_ref[...], preferred_element_type=F32).astype(o_ref.dtype)


def matmul(a, b, out_dtype, tm=512, tn=512):
    m, k = a.shape
    _, n = b.shape
    return pl.pallas_call(
        _mm_kernel,
        grid=(m // tm, n // tn),
        in_specs=[pl.BlockSpec((tm, k), lambda i, j: (i, 0)),
                  pl.BlockSpec((k, tn), lambda i, j: (0, j))],
        out_specs=pl.BlockSpec((tm, tn), lambda i, j: (i, j)),
        out_shape=jax.ShapeDtypeStruct((m, n), out_dtype),
        compiler_params=_params("parallel", "parallel"),
        name="matmul",
    )(a, b)


def _mm2_kernel(a1_ref, a2_ref, b1_ref, b2_ref, o_ref):
    acc = jnp.dot(a1_ref[...], b1_ref[...], preferred_element_type=F32)
    acc += jnp.dot(a2_ref[...], b2_ref[...], preferred_element_type=F32)
    o_ref[...] = acc.astype(o_ref.dtype)


def matmul_cat(a1, a2, b, out_dtype, tm=512, tn=512):
    m, k1 = a1.shape
    _, k2 = a2.shape
    assert k1 == k2 and b.shape[0] == k1 + k2
    n = b.shape[1]
    return pl.pallas_call(
        _mm2_kernel,
        grid=(m // tm, n // tn),
        in_specs=[pl.BlockSpec((tm, k1), lambda i, j: (i, 0)),
                  pl.BlockSpec((tm, k2), lambda i, j: (i, 0)),
                  pl.BlockSpec((k1, tn), lambda i, j: (0, j)),
                  pl.BlockSpec((k2, tn), lambda i, j: (1, j))],
        out_specs=pl.BlockSpec((tm, tn), lambda i, j: (i, j)),
        out_shape=jax.ShapeDtypeStruct((m, n), out_dtype),
        compiler_params=_params("parallel", "parallel"),
        name="matmul_cat",
    )(a1, a2, b, b)


def _shift_rows(x, prev, d):
    y = pltpu.roll(x, d, 0)
    row = lax.broadcasted_iota(jnp.int32, x.shape, 0)
    for r in range(d):
        y = jnp.where(row == r, prev[SUBLANES - d + r:SUBLANES - d + r + 1, :], y)
    return y


def _ffn_up_kernel(h_ref, wg_ref, wv_ref, cg_ref, cv_ref, o_ref, hg_ref, hv_ref):
    i = pl.program_id(1)

    @pl.when(i == 0)
    def _():
        hg_ref[...] = jnp.zeros_like(hg_ref)
        hv_ref[...] = jnp.zeros_like(hv_ref)

    h = h_ref[...]

    def conv(w_ref, c_ref, halo_ref):
        raw = jnp.dot(h, w_ref[...], preferred_element_type=F32)
        prev = halo_ref[...]
        c = c_ref[...]
        out = (c[2:3, :] * raw + c[1:2, :] * _shift_rows(raw, prev, 1)
               + c[0:1, :] * _shift_rows(raw, prev, 2))
        halo_ref[...] = raw[raw.shape[0] - SUBLANES:, :]
        return out

    gate = conv(wg_ref, cg_ref, hg_ref)
    val = conv(wv_ref, cv_ref, hv_ref)
    o_ref[...] = (jax.nn.gelu(gate, approximate=True) * val).astype(o_ref.dtype)


def ffn_up(h, w_up, conv_w, tm=512, tn=512):
    s, d = h.shape
    f = w_up.shape[1] // 2
    nj = f // tn
    return pl.pallas_call(
        _ffn_up_kernel,
        grid=(nj, s // tm),
        in_specs=[pl.BlockSpec((tm, d), lambda j, i: (i, 0)),
                  pl.BlockSpec((d, tn), lambda j, i: (0, j)),
                  pl.BlockSpec((d, tn), lambda j, i: (0, j + nj)),
                  pl.BlockSpec((FFN_CONV, tn), lambda j, i: (0, j)),
                  pl.BlockSpec((FFN_CONV, tn), lambda j, i: (0, j + nj))],
        out_specs=pl.BlockSpec((tm, tn), lambda j, i: (i, j)),
        out_shape=jax.ShapeDtypeStruct((s, f), BF16),
        scratch_shapes=[pltpu.VMEM((SUBLANES, tn), F32), pltpu.VMEM((SUBLANES, tn), F32)],
        compiler_params=_params("parallel", "arbitrary"),
        name="ffn_up",
    )(h, w_up, w_up, conv_w, conv_w)


def _dot_nt(a, b):
    return lax.dot_general(a, b, (((1,), (1,)), ((), ())), preferred_element_type=F32)


def _moba_kernel(slope_ref, q_ref, k_ref, v_ref, o_ref, kbar_ref, s0_buf, s1_buf, p0_buf, p1_buf, *, n_blk):
    hd = pl.program_id(0)
    i = pl.program_id(1)
    bs = MOBA_BLOCK
    slope = slope_ref[hd]

    @pl.when(i == 0)
    def _():
        def body(j, c):
            kb = k_ref[pl.ds(pl.multiple_of(j * bs, bs), bs), :].astype(F32)
            kbar_ref[pl.ds(j, 1), :] = jnp.mean(kb, axis=0, keepdims=True)
            return c
        lax.fori_loop(0, n_blk, body, 0)

    q = q_ref[...]
    kbar = kbar_ref[...]
    kb_hi = kbar.astype(BF16)
    kb_lo = (kbar - kb_hi.astype(F32)).astype(BF16)
    gate = _dot_nt(q, kb_hi) + _dot_nt(q, kb_lo)
    blk = lax.broadcasted_iota(jnp.int32, gate.shape, 1)
    gate = jnp.where(blk < i, gate, -jnp.inf)
    sel = []
    for _ in range(MOBA_TOPK):
        m = jnp.max(gate, axis=-1, keepdims=True)
        cand = (gate == m) & (gate > -jnp.inf)
        idx = jnp.min(jnp.where(cand, blk, n_blk), axis=-1, keepdims=True)
        sel.append(idx)
        gate = jnp.where(blk == idx, -jnp.inf, gate)

    slope2 = slope * LOG2E
    col = lax.broadcasted_iota(jnp.int32, (bs, bs), 1)
    row = lax.broadcasted_iota(jnp.int32, (bs, bs), 0)
    cb_own = slope2 * col.astype(F32)
    cb_pair = slope2 * lax.broadcasted_iota(jnp.int32, (bs, 2 * bs), 1).astype(F32)

    i0 = pl.multiple_of(i * bs, bs)
    s = _dot_nt(q, k_ref[pl.ds(i0, bs), :]) + cb_own
    s = jnp.where(col <= row, s, NEG_BIG)
    m0 = jnp.max(s, axis=-1, keepdims=True)
    p = jnp.exp2(s - m0)
    l0 = jnp.sum(p, axis=-1, keepdims=True)
    acc0 = jnp.dot(p.astype(BF16), v_ref[pl.ds(i0, bs), :], preferred_element_type=F32)

    def qbias(j):
        picked = (sel[0] == j) | (sel[1] == j) | (sel[2] == j)
        return jnp.where(picked, 0.0, NEG_BIG)

    def pair_rows(ref, t):
        return ref[pl.ds(pl.multiple_of(t * (2 * bs), 2 * bs), 2 * bs), :]

    n_it = (i + 1) // 2
    t_last = jnp.maximum(n_it - 1, 0)

    s_buf = (s0_buf, s1_buf)
    p_buf = (p0_buf, p1_buf)
    s_buf[0][...] = _dot_nt(q, pair_rows(k_ref, 0))
    p_buf[1][...] = jnp.zeros((bs, 2 * bs), BF16)

    def phase(t, carry, cur):
        m_i, l_i, acc = carry
        s_buf[1 - cur][...] = _dot_nt(q, pair_rows(k_ref, jnp.minimum(t + 1, t_last)))
        pv = jnp.dot(p_buf[1 - cur][...], pair_rows(v_ref, jnp.clip(t - 1, 0, t_last)),
                     preferred_element_type=F32)
        ja = 2 * t
        s = s_buf[cur][...] + cb_pair
        sa, sb = s[:, :bs], s[:, bs:]
        off = slope2 * ((ja - i) * bs).astype(F32)
        qa, qb = qbias(ja) + off, qbias(ja + 1) + off
        m_new = jnp.maximum(m_i, jnp.maximum(jnp.max(sa, axis=-1, keepdims=True) + qa,
                                             jnp.max(sb, axis=-1, keepdims=True) + qb))
        pa = jnp.exp2(sa + (qa - m_new))
        pb = jnp.exp2(sb + (qb - m_new))
        alpha = jnp.exp2(m_i - m_new)
        l_new = (alpha * l_i + jnp.sum(pa, axis=-1, keepdims=True)
                 + jnp.sum(pb, axis=-1, keepdims=True))
        p_buf[cur][...] = jnp.concatenate([pa, pb], axis=1).astype(BF16)
        return m_new, l_new, alpha * (acc + pv)

    def body(u, carry):
        return phase(2 * u + 1, phase(2 * u, carry, 0), 1)

    n_trip = (n_it + 1) // 2
    _, l_f, acc_f = lax.fori_loop(0, n_trip, body, (m0, l0, acc0))
    acc_f = acc_f + jnp.dot(p_buf[1][...], pair_rows(v_ref, jnp.clip(2 * n_trip - 1, 0, t_last)),
                            preferred_element_type=F32)
    o_ref[...] = (acc_f / l_f).astype(o_ref.dtype)


def moba_attention(proj, n_heads):
    s = proj.shape[0]
    dh, bs = MOBA_HEAD_DIM, MOBA_BLOCK
    n_blk = s // bs
    slopes = 2.0 ** (-8.0 * jnp.arange(1, n_heads + 1, dtype=F32) / n_heads)
    grid_spec = pltpu.PrefetchScalarGridSpec(
        num_scalar_prefetch=1,
        grid=(n_heads, n_blk),
        in_specs=[pl.BlockSpec((bs, dh), lambda h, i, sl: (i, h)),
                  pl.BlockSpec((s, dh), lambda h, i, sl: (0, n_heads + h)),
                  pl.BlockSpec((s, dh), lambda h, i, sl: (0, 2 * n_heads + h))],
        out_specs=pl.BlockSpec((bs, dh), lambda h, i, sl: (i, h)),
        scratch_shapes=[pltpu.VMEM((n_blk, dh), F32),
                        pltpu.VMEM((bs, 2 * bs), F32), pltpu.VMEM((bs, 2 * bs), F32),
                        pltpu.VMEM((bs, 2 * bs), BF16), pltpu.VMEM((bs, 2 * bs), BF16)],
    )
    return pl.pallas_call(
        functools.partial(_moba_kernel, n_blk=n_blk),
        grid_spec=grid_spec,
        out_shape=jax.ShapeDtypeStruct((s, n_heads * dh), BF16),
        compiler_params=_params("parallel", "arbitrary"),
        name="moba",
    )(slopes, proj, proj, proj)


def _softplus(z):
    return jnp.maximum(z, 0.0) + jnp.log1p(jnp.exp(-jnp.abs(z)))


def _sb_kernel(q_ref, k_ref, v_ref, o_ref, z0_buf, z1_buf, w0_buf, w1_buf, *, tb):
    i = pl.program_id(1)
    q = q_ref[...]
    row = lax.broadcasted_iota(jnp.int32, (tb, tb), 0)
    col = lax.broadcasted_iota(jnp.int32, (tb, tb), 1)
    tri = jnp.where(row >= col, 1.0, 0.0).astype(BF16)
    past = col < row

    def weights(z2, later, masked):
        nz = -z2
        lk = jnp.minimum(nz, 0.0) - jnp.log2(1.0 + jnp.exp2(jnp.minimum(z2, nz)))
        if masked:
            lk = jnp.where(past, lk, 0.0)
        c = jnp.dot(lk.astype(BF16), tri, preferred_element_type=F32)
        w = jnp.exp2(z2 + c + later)
        if masked:
            w = jnp.where(past, w, 0.0)
        return w.astype(BF16), jnp.sum(lk, axis=-1, keepdims=True)

    i0 = pl.multiple_of(i * tb, tb)
    w0, later0 = weights(_dot_nt(q, k_ref[pl.ds(i0, tb), :]), jnp.zeros((tb, 1), F32), True)
    acc0 = jnp.dot(w0, v_ref[pl.ds(i0, tb), :], preferred_element_type=F32)

    n_trip = i // 4
    t_last = jnp.maximum(2 * n_trip - 1, 0)
    z_buf = (z0_buf, z1_buf)
    w_buf = (w0_buf, w1_buf)

    def pair_rows(ref, t):
        j0 = jnp.maximum(i - 2 - 2 * t, 0) * tb
        return ref[pl.ds(pl.multiple_of(j0, tb), 2 * tb), :]

    z_buf[0][...] = _dot_nt(q, pair_rows(k_ref, 0))
    w_buf[1][...] = jnp.zeros((tb, 2 * tb), BF16)

    def pair(t, carry, cur):
        acc, later = carry
        z_buf[1 - cur][...] = _dot_nt(q, pair_rows(k_ref, jnp.minimum(t + 1, t_last)))
        acc = acc + jnp.dot(w_buf[1 - cur][...], pair_rows(v_ref, jnp.maximum(t - 1, 0)),
                            preferred_element_type=F32)
        z2 = z_buf[cur][...]
        w_hi, tot_hi = weights(z2[:, tb:], later, False)
        later = later + tot_hi
        w_lo, tot_lo = weights(z2[:, :tb], later, False)
        w_buf[cur][...] = jnp.concatenate([w_lo, w_hi], axis=1)
        return acc, later + tot_lo

    def trip(u, carry):
        return pair(2 * u + 1, pair(2 * u, carry, 0), 1)

    acc, later = lax.fori_loop(0, n_trip, trip, (acc0, later0))
    acc = acc + jnp.dot(w_buf[1][...], pair_rows(v_ref, t_last), preferred_element_type=F32)

    n_left = i - 4 * n_trip

    def single(t, carry):
        acc, later = carry
        j0 = pl.multiple_of((n_left - 1 - t) * tb, tb)
        w, tot = weights(_dot_nt(q, k_ref[pl.ds(j0, tb), :]), later, False)
        return acc + jnp.dot(w, v_ref[pl.ds(j0, tb), :], preferred_element_type=F32), later + tot

    acc, _ = lax.fori_loop(0, n_left, single, (acc, later))
    o_ref[...] = acc.astype(o_ref.dtype)


def sb_attention(proj, n_heads, col0, tb=256):
    s = proj.shape[0]
    dh = SB_HEAD_DIM
    c0 = col0 // dh
    return pl.pallas_call(
        functools.partial(_sb_kernel, tb=tb),
        grid=(n_heads, s // tb),
        in_specs=[pl.BlockSpec((tb, dh), lambda h, i: (i, c0 + h)),
                  pl.BlockSpec((s, dh), lambda h, i: (0, c0 + n_heads + h)),
                  pl.BlockSpec((s, dh), lambda h, i: (0, c0 + 2 * n_heads + h))],
        out_specs=pl.BlockSpec((tb, dh), lambda h, i: (i, h)),
        out_shape=jax.ShapeDtypeStruct((s, n_heads * dh), BF16),
        scratch_shapes=[pltpu.VMEM((tb, 2 * tb), F32), pltpu.VMEM((tb, 2 * tb), F32),
                        pltpu.VMEM((tb, 2 * tb), BF16), pltpu.VMEM((tb, 2 * tb), BF16)],
        compiler_params=_params("parallel", "arbitrary"),
        name="stickbreak",
    )(proj, proj, proj)


RWKV_W = 2048
LORA_W = 640
HEAD_GROUP = 256
RWKV_CHUNK = 64


def _sigmoid(x):
    return 1.0 / (1.0 + jnp.exp(-x))


def _head_ones(n):
    r = lax.broadcasted_iota(jnp.int32, (n, n), 0) // RWKV_HEAD
    c = lax.broadcasted_iota(jnp.int32, (n, n), 1) // RWKV_HEAD
    return jnp.where(r == c, 1.0, 0.0).astype(BF16)


def _head_sum(x):
    e = _head_ones(HEAD_GROUP)
    hi = x.astype(BF16)
    lo = (x - hi.astype(F32)).astype(BF16)
    outs = []
    for c in range(x.shape[1] // HEAD_GROUP):
        sl = slice(c * HEAD_GROUP, (c + 1) * HEAD_GROUP)
        outs.append(jnp.dot(hi[:, sl], e, preferred_element_type=F32)
                    + jnp.dot(lo[:, sl], e, preferred_element_type=F32))
    return jnp.concatenate(outs, axis=-1)


def _token_shift(x_ref, halo_ref, mu_ref, first):
    x = x_ref[...]
    prev = jnp.where(first, 0.0, halo_ref[...])
    return x + (_shift_rows(x, prev, 1) - x) * mu_ref[...]


def _rwkv_prep_kernel(*refs, vres):
    (r_ref, k_ref, v_ref, lo_ref, rh_ref, kh_ref, vh_ref, loh_ref,
     mur_ref, muk_ref, muv_ref, mulo_ref,
     w0_ref, w2_ref, a0_ref, a2_ref, v0_ref, v2_ref, g2_ref, kk_ref, ka_ref, rk_ref) = refs[:22]
    rest = refs[22:]
    if vres:
        vf_ref, rest = rest[0], rest[1:]
    ro_ref, lw_ref, ko_ref, vo_ref, na_ref, bb_ref, g_ref, bo_ref = rest
    first = pl.program_id(0) == 0
    r = _token_shift(r_ref, rh_ref, mur_ref, first)
    k = _token_shift(k_ref, kh_ref, muk_ref, first)
    v = _token_shift(v_ref, vh_ref, muv_ref, first)
    lo = _token_shift(lo_ref, loh_ref, mulo_ref, first)
    w_lo, a_lo, v_lo, g_lo = lo[:, 0:128], lo[:, 128:256], lo[:, 256:384], lo[:, 384:640]

    def lora(x, w_ref):
        return jnp.dot(x.astype(BF16), w_ref[...], preferred_element_type=F32)

    w_log = -_softplus(-(w0_ref[...] + lora(jnp.tanh(w_lo), w2_ref))) - 0.5
    lw_ref[...] = -jnp.exp(w_log)
    if vres:
        v = v + (vf_ref[...] - v) * _sigmoid(v0_ref[...] + lora(v_lo, v2_ref))
    a = _sigmoid(a0_ref[...] + lora(a_lo, a2_ref))
    g_ref[...] = lora(_sigmoid(g_lo), g2_ref)
    kk = k * kk_ref[...]
    kk = kk / jnp.maximum(jnp.sqrt(_head_sum(kk * kk)), 1e-12)
    k2 = k * (1.0 + (a - 1.0) * ka_ref[...])
    ro_ref[...] = r
    ko_ref[...] = k2
    vo_ref[...] = v
    na_ref[...] = -kk
    bb_ref[...] = kk * a
    bo_ref[...] = _head_sum(r * k2 * rk_ref[...]) * v


def rwkv_prep(proj, col0, mu, w0, w2p, a0, a2p, v0, v2p, g2, k_k, k_a, r_k, v_first, tm=128):
    s = proj.shape[0]
    w = RWKV_W
    vres = v_first is not None
    c_r, c_lo = col0 // w, (col0 + 3 * w) // LORA_W
    hb = tm // SUBLANES

    def main(width, cidx):
        return pl.BlockSpec((tm, width), lambda i: (i, cidx))

    def halo(width, cidx):
        return pl.BlockSpec((SUBLANES, width), lambda i: (jnp.maximum(i * hb - 1, 0), cidx))

    def vec(width, cidx=0):
        return pl.BlockSpec((1, width), lambda i: (0, cidx))

    def full(shape):
        return pl.BlockSpec(shape, lambda i: (0, 0))

    mu2 = mu.reshape(1, -1)
    r1 = lambda t: t.reshape(1, -1)
    in_specs = [main(w, c_r), main(w, c_r + 1), main(w, c_r + 2), main(LORA_W, c_lo),
                halo(w, c_r), halo(w, c_r + 1), halo(w, c_r + 2), halo(LORA_W, c_lo),
                vec(w, c_r), vec(w, c_r + 1), vec(w, c_r + 2), vec(LORA_W, c_lo),
                vec(w), full((128, w)), vec(w), full((128, w)), vec(w), full((128, w)),
                full((256, w)), vec(w), vec(w), vec(w)]
    args = [proj] * 8 + [mu2] * 4 + [r1(w0), w2p, r1(a0), a2p, r1(v0), v2p, g2, r1(k_k), r1(k_a), r1(r_k)]
    if vres:
        in_specs.append(main(w, 0))
        args.append(v_first)
    out = pl.BlockSpec((tm, w), lambda i: (i, 0))
    return pl.pallas_call(
        functools.partial(_rwkv_prep_kernel, vres=vres),
        grid=(s // tm,),
        in_specs=in_specs,
        out_specs=[out] * 8,
        out_shape=[jax.ShapeDtypeStruct((s, w), F32)] * 8,
        compiler_params=_params("parallel"),
        name="rwkv_prep",
    )(*args)


def _cumsum_rows(x):
    n = x.shape[0]
    row = lax.broadcasted_iota(jnp.int32, x.shape, 0)
    d = 1
    while d < n:
        x = x + jnp.where(row >= d, pltpu.roll(x, d, 0), 0.0)
        d *= 2
    return x


def _rwkv_chunk(r, lw, k, v, a, b, state):
    L, n2 = r.shape
    nh = n2 // RWKV_HEAD
    J = nh * L
    bd = lambda x: x.astype(BF16)
    mm = lambda x, y: jnp.dot(bd(x), bd(y), preferred_element_type=F32)
    lg = _cumsum_rows(lw)
    tot = lg[L - 1:L, :]
    e_neg = jnp.exp(-lg)
    e_hat = jnp.exp(tot - lg)
    a_t = a * jnp.exp(lg - lw)
    r_t = r * jnp.exp(lg)
    b_i, k_i = b * e_neg, k * e_neg
    b_hat, k_hat = b * e_hat, k * e_hat

    head = lax.broadcasted_iota(jnp.int32, (L, n2), 1) // RWKV_HEAD

    def stack(x):
        return jnp.concatenate([jnp.where(head == h, x, 0.0) for h in range(nh)], axis=0)

    def fold(x):
        out = x[:L]
        for h in range(1, nh):
            out = out + x[h * L:(h + 1) * L]
        return out

    row = lax.broadcasted_iota(jnp.int32, (J, J), 0)
    col = lax.broadcasted_iota(jnp.int32, (J, J), 1)
    same = (row // L) == (col // L)
    strict = same & (col < row)
    incl = same & (col <= row)
    eye = jnp.where(row == col, 1.0, 0.0)

    x_a, x_r = stack(a_t), stack(r_t)
    ybk = jnp.concatenate([b_i] * nh + [k_i] * nh, axis=0)
    pa = _dot_nt(bd(x_a), bd(ybk))
    pr = _dot_nt(bd(x_r), bd(ybk))
    m_ab = jnp.where(strict, pa[:, :J], 0.0)
    m_ak = jnp.where(strict, pa[:, J:], 0.0)
    m_rb = jnp.where(incl, pr[:, :J], 0.0)
    m_rk = jnp.where(incl, pr[:, J:], 0.0)

    t_inv, pw = eye + m_ab, m_ab
    for _ in range(int(math.log2(L)) - 1):
        pw = mm(pw, pw)
        t_inv = mm(t_inv, eye + pw)

    v_st = stack(v)
    w1 = mm(m_ak, v_st)
    tw = mm(t_inv, jnp.concatenate([x_a, w1], axis=1))
    ta, w0 = fold(tw[:, :n2]), fold(tw[:, n2:])
    gs = _dot_nt(bd(jnp.concatenate([ta, r_t], axis=0)), bd(state))
    u = gs[:L] + w0
    yj = mm(jnp.concatenate([m_rb, m_rk], axis=1), jnp.concatenate([stack(u), v_st], axis=0))
    y = gs[L:] + fold(yj)

    lhs_t = jnp.concatenate([jnp.concatenate([ta, jnp.zeros_like(ta)], axis=0).T,
                             jnp.concatenate([w0, v], axis=0).T], axis=0)
    pq = mm(lhs_t, jnp.concatenate([b_hat, k_hat], axis=0))
    ch_r = lax.broadcasted_iota(jnp.int32, (n2, n2), 0)
    ch_c = lax.broadcasted_iota(jnp.int32, (n2, n2), 1)
    head_blk = (ch_r // RWKV_HEAD) == (ch_c // RWKV_HEAD)
    p_mat = jnp.where(head_blk, pq[:n2], 0.0) + jnp.where(ch_r == ch_c, jnp.exp(tot), 0.0)
    q_mat = jnp.where(head_blk, pq[n2:], 0.0)
    return y, mm(state, p_mat) + q_mat


def _rwkv_rec_kernel(r_ref, lw_ref, k_ref, v_ref, a_ref, b_ref, y_ref, st_ref, *, n_groups, n_chunks):
    gw = st_ref.shape[-1]

    @pl.when(pl.program_id(1) == 0)
    def _():
        st_ref[...] = jnp.zeros_like(st_ref)

    def body(c, carry):
        rows = pl.ds(pl.multiple_of(c * RWKV_CHUNK, RWKV_CHUNK), RWKV_CHUNK)
        for p in range(n_groups):
            lanes = slice(p * gw, (p + 1) * gw)
            ld = lambda ref: ref[rows, lanes]
            y, st = _rwkv_chunk(ld(r_ref), ld(lw_ref), ld(k_ref), ld(v_ref), ld(a_ref), ld(b_ref), st_ref[p])
            y_ref[rows, lanes] = y
            st_ref[p] = st
        return carry

    lax.fori_loop(0, n_chunks, body, 0)


def rwkv_recurrence(r, lw, k, v, a, b, tt=512, n_groups=2, heads_per_group=4):
    s, w = r.shape
    gw = heads_per_group * RWKV_HEAD
    wb = n_groups * gw
    blk = pl.BlockSpec((tt, wb), lambda h, t: (t, h))
    return pl.pallas_call(
        functools.partial(_rwkv_rec_kernel, n_groups=n_groups, n_chunks=tt // RWKV_CHUNK),
        grid=(w // wb, s // tt),
        in_specs=[blk] * 6,
        out_specs=blk,
        out_shape=jax.ShapeDtypeStruct((s, w), F32),
        scratch_shapes=[pltpu.VMEM((n_groups, gw, gw), F32)],
        compiler_params=_params("arbitrary", "arbitrary"),
        name="rwkv_rec",
    )(r, lw, k, v, a, b)


def _rwkv_post_kernel(y_ref, g_ref, bo_ref, lw_ref, lb_ref, o_ref):
    y = y_ref[...]
    inv_n = 1.0 / RWKV_HEAD
    mu = _head_sum(y) * inv_n
    yc = y - mu
    var = _head_sum(yc * yc) * inv_n
    yn = yc * lax.rsqrt(var + GN_EPS) * lw_ref[...] + lb_ref[...]
    o_ref[...] = ((yn + bo_ref[...]) * g_ref[...]).astype(o_ref.dtype)


def rwkv_post(y, g, bonus, lnx_w, lnx_b, tm=256):
    s, w = y.shape
    row = pl.BlockSpec((tm, w), lambda i: (i, 0))
    vec = pl.BlockSpec((1, w), lambda i: (0, 0))
    return pl.pallas_call(
        _rwkv_post_kernel,
        grid=(s // tm,),
        in_specs=[row, row, row, vec, vec],
        out_specs=row,
        out_shape=jax.ShapeDtypeStruct((s, w), BF16),
        compiler_params=_params("parallel"),
        name="rwkv_post",
    )(y, g, bonus, lnx_w.reshape(1, w), lnx_b.reshape(1, w))


def _expm1(y):
    u = jnp.exp(y)
    near = (u - 1.0) * y / jnp.log(u)
    return jnp.where(u == 1.0, y, jnp.where(y < -0.5, u - 1.0, near))


def _lru_kernel(gate_ref, x_ref, xh_ref, cw_ref, cb_ref, gw_ref, gb_ref, lam_ref, o_ref, h_ref, *, nb):
    t = pl.program_id(1)

    @pl.when(t == 0)
    def _():
        h_ref[...] = jnp.zeros_like(h_ref)

    x = x_ref[...]
    prev = jnp.where(t == 0, 0.0, xh_ref[...])
    cw = cw_ref[...]
    xb = cw[3:4, :] * x + cb_ref[...]
    for d in range(1, LRU_CONV):
        xb = xb + cw[3 - d:4 - d, :] * _shift_rows(x, prev, d)
    xb16 = xb.astype(BF16)
    g0, g1 = [], []
    for n in range(nb):
        xs = xb16[:, n * LANES:(n + 1) * LANES]
        g0.append(jnp.dot(xs, gw_ref[0, n], preferred_element_type=F32))
        g1.append(jnp.dot(xs, gw_ref[1, n], preferred_element_type=F32))
    gb = gb_ref[...]
    g0 = jnp.concatenate(g0, axis=-1) + gb[0:1, :]
    g1 = jnp.concatenate(g1, axis=-1) + gb[1:2, :]
    log_a = -LRU_C * _sigmoid(g0) * _softplus(-lam_ref[...])
    a = jnp.exp(log_a)
    b = jnp.sqrt(-_expm1(2.0 * log_a)) * _sigmoid(g1) * xb
    n = x.shape[0]
    row = lax.broadcasted_iota(jnp.int32, x.shape, 0)
    d = 1
    while d < n:
        keep = row >= d
        a_s = jnp.where(keep, pltpu.roll(a, d, 0), 1.0)
        b_s = jnp.where(keep, pltpu.roll(b, d, 0), 0.0)
        b = a * b_s + b
        a = a * a_s
        d *= 2
    h = b + a * h_ref[SUBLANES - 1:SUBLANES, :]
    h_ref[...] = h[n - SUBLANES:, :]
    o_ref[...] = (h * jax.nn.gelu(gate_ref[...], approximate=True)).astype(o_ref.dtype)


def rg_lru(proj, conv_w, conv_b, gate_w, gate_b, lam, tm=256, tc=512):
    s = proj.shape[0]
    w = conv_w.shape[1]
    nj = w // tc
    nb = tc // LANES
    hb = tm // SUBLANES
    return pl.pallas_call(
        functools.partial(_lru_kernel, nb=nb),
        grid=(nj, s // tm),
        in_specs=[pl.BlockSpec((tm, tc), lambda j, t: (t, j)),
                  pl.BlockSpec((tm, tc), lambda j, t: (t, nj + j)),
                  pl.BlockSpec((SUBLANES, tc), lambda j, t: (jnp.maximum(t * hb - 1, 0), nj + j)),
                  pl.BlockSpec((LRU_CONV, tc), lambda j, t: (0, j)),
                  pl.BlockSpec((1, tc), lambda j, t: (0, j)),
                  pl.BlockSpec((2, nb, LANES, LANES), lambda j, t: (0, j, 0, 0)),
                  pl.BlockSpec((2, tc), lambda j, t: (0, j)),
                  pl.BlockSpec((1, tc), lambda j, t: (0, j))],
        out_specs=pl.BlockSpec((tm, tc), lambda j, t: (t, j)),
        out_shape=jax.ShapeDtypeStruct((s, w), BF16),
        scratch_shapes=[pltpu.VMEM((SUBLANES, tc), F32)],
        compiler_params=_params("arbitrary", "arbitrary"),
        name="rg_lru",
    )(proj, proj, proj, conv_w, conv_b.reshape(1, w), gate_w.astype(BF16), gate_b, lam.reshape(1, w))


def even_mixer(h, w_in, w_out):
    half = w_out.shape[0] // 2
    n_moba = half // MOBA_HEAD_DIM
    n_sb = half // SB_HEAD_DIM
    col = jnp.arange(w_in.shape[1])
    qscale = jnp.where(col < half, MOBA_HEAD_DIM ** -0.5 * LOG2E,
                       jnp.where((col >= 3 * half) & (col < 4 * half), SB_HEAD_DIM ** -0.5 * LOG2E, 1.0))
    proj = matmul(h, (w_in * qscale.astype(F32)).astype(BF16), BF16)
    oa = moba_attention(proj, n_moba)
    ob = sb_attention(proj, n_sb, 3 * n_moba * MOBA_HEAD_DIM)
    return matmul_cat(oa, ob, w_out.astype(BF16), F32)


def _pad_cols(t, width):
    return jnp.pad(t, [(0, 0)] * (t.ndim - 1) + [(0, width - t.shape[-1])])


def _odd_layout(t, n_c, vres):
    w = RWKV_W
    sizes = [96, 96] + ([64] if vres else []) + [256]
    segs, c = [], 3 * w
    for sz in sizes:
        segs.append(_pad_cols(t[..., c:c + sz], LANES if sz < LANES else sz))
        c += sz
    if not vres:
        segs.insert(2, jnp.zeros(t.shape[:-1] + (LANES,), t.dtype))
    assert c == n_c
    return jnp.concatenate([t[..., n_c:], t[..., :3 * w]] + segs, axis=-1)


def _pad_rows(t, rows):
    return jnp.pad(t, ((0, rows - t.shape[0]), (0, 0)))


def odd_mixer(h, v_first, p):
    w = RWKV_W
    vres = "v2" in p
    n_c = p["shift_mu"].shape[0]
    w_in = _odd_layout(p["w_in"], n_c, vres).astype(BF16)
    lru_w = p["w_in"].shape[1] - n_c
    mu = _odd_layout(jnp.concatenate([p["shift_mu"], jnp.zeros((lru_w,), F32)]), n_c, vres)
    proj = matmul(h, w_in, F32, tn=LORA_W)
    zeros_w = jnp.zeros((w,), F32)
    r, lw, k, v, na, bb, g, bonus = rwkv_prep(
        proj, lru_w, mu, p["w0"], _pad_rows(p["w2"], LANES).astype(BF16),
        p["a0"], _pad_rows(p["a2"], LANES).astype(BF16),
        p["v0"] if vres else zeros_w,
        _pad_rows(p["v2"], LANES).astype(BF16) if vres else jnp.zeros((LANES, w), BF16),
        p["g2"].astype(BF16), p["k_k"], p["k_a"], p["r_k"].reshape(-1), v_first)
    if v_first is None:
        v_first = v
    y = rwkv_recurrence(r, lw, k, v, na, bb)
    c_out = rwkv_post(y, g, bonus, p["lnx_w"], p["lnx_b"])
    d_out = rg_lru(proj, p["conv_w"], p["conv_b"], p["gate_w"], p["gate_b"], p["lru_lambda"])
    return matmul_cat(c_out, d_out, p["w_out"].astype(BF16), F32), v_first


_ODD_NAMES = ("w_in", "w_out", "shift_mu", "w0", "w2", "a0", "a2", "g2", "k_k", "k_a", "r_k",
              "lnx_w", "lnx_b", "conv_w", "conv_b", "gate_w", "gate_b", "lru_lambda")


def _forward(x, layers):
    x2 = x.reshape(x.shape[-2], x.shape[-1])
    h = rmsnorm(x2, layers[0]["norms"][0])
    v_first = None
    for li, p in enumerate(layers):
        g = p["norms"]
        if "shift_mu" in p:
            mix, v_first = odd_mixer(h, v_first, p)
        else:
            mix = even_mixer(h, p["w_in"], p["w_out"])
        x2, h = add_rmsnorm(x2, mix, g[1], g[2])
        act = ffn_up(h, p["ffn_up"].astype(BF16), p["ffn_conv"])
        y = matmul(act, p["ffn_down"].astype(BF16), F32)
        g_next = layers[li + 1]["norms"][0] if li + 1 < len(layers) else g[3]
        x2, h = add_rmsnorm(x2, y, g[3], g_next)
    return x2.reshape(x.shape)


def kernel(x, l0_norms, l0_w_in, l0_w_out, l0_ffn_up, l0_ffn_conv, l0_ffn_down, l1_norms, l1_w_in, l1_w_out, l1_shift_mu, l1_w0, l1_w2, l1_a0, l1_a2, l1_g2, l1_k_k, l1_k_a, l1_r_k, l1_lnx_w, l1_lnx_b, l1_conv_w, l1_conv_b, l1_gate_w, l1_gate_b, l1_lru_lambda, l1_ffn_up, l1_ffn_conv, l1_ffn_down, l2_norms, l2_w_in, l2_w_out, l2_ffn_up, l2_ffn_conv, l2_ffn_down, l3_norms, l3_w_in, l3_w_out, l3_shift_mu, l3_w0, l3_w2, l3_a0, l3_a2, l3_v0, l3_v2, l3_g2, l3_k_k, l3_k_a, l3_r_k, l3_lnx_w, l3_lnx_b, l3_conv_w, l3_conv_b, l3_gate_w, l3_gate_b, l3_lru_lambda, l3_ffn_up, l3_ffn_conv, l3_ffn_down):
    a = locals()
    layers = []
    for li in range(4):
        pre = "l%d_" % li
        layers.append({k[len(pre):]: v for k, v in a.items() if k.startswith(pre)})
    return _forward(x, layers)
```

```python
import functools
import math

import jax
import jax.numpy as jnp
from jax import lax
from jax.experimental import pallas as pl
from jax.experimental.pallas import tpu as pltpu

F32 = jnp.float32
BF16 = jnp.bfloat16

V7X_VMEM_BYTES = 64 * 1024 * 1024
VMEM_LIMIT = 56 * 1024 * 1024
LANES = 128
SUBLANES = 8

RMS_EPS = 1e-6
GN_EPS = 64e-5
MOBA_HEAD_DIM = 128
MOBA_BLOCK = 256
MOBA_TOPK = 3
SB_HEAD_DIM = 256
RWKV_HEAD = 64
LRU_BLOCKS = 16
LRU_CONV = 4
LRU_C = 8.0
FFN_CONV = 3
NEG_BIG = -1e30
LOG2E = 1.4426950408889634
LN2 = 0.6931471805599453


def _params(*sem):
    return pltpu.CompilerParams(dimension_semantics=sem, vmem_limit_bytes=VMEM_LIMIT)


def _rms(x, g):
    return x * lax.rsqrt(jnp.mean(x * x, axis=-1, keepdims=True) + RMS_EPS) * g


def _rmsnorm_kernel(x_ref, g_ref, o_ref):
    o_ref[...] = _rms(x_ref[...], g_ref[...]).astype(o_ref.dtype)


def rmsnorm(x, g, tm=256):
    s, d = x.shape
    return pl.pallas_call(
        _rmsnorm_kernel,
        grid=(s // tm,),
        in_specs=[pl.BlockSpec((tm, d), lambda i: (i, 0)),
                  pl.BlockSpec((1, d), lambda i: (0, 0))],
        out_specs=pl.BlockSpec((tm, d), lambda i: (i, 0)),
        out_shape=jax.ShapeDtypeStruct((s, d), BF16),
        compiler_params=_params("parallel"),
        name="rmsnorm",
    )(x, g.reshape(1, d))


def _add_rmsnorm_kernel(x_ref, y_ref, g1_ref, g2_ref, xo_ref, ho_ref):
    xn = x_ref[...] + _rms(y_ref[...], g1_ref[...])
    xo_ref[...] = xn
    ho_ref[...] = _rms(xn, g2_ref[...]).astype(ho_ref.dtype)


def add_rmsnorm(x, y, g_post, g_pre, tm=256):
    s, d = x.shape
    row = pl.BlockSpec((tm, d), lambda i: (i, 0))
    vec = pl.BlockSpec((1, d), lambda i: (0, 0))
    return pl.pallas_call(
        _add_rmsnorm_kernel,
        grid=(s // tm,),
        in_specs=[row, row, vec, vec],
        out_specs=[row, row],
        out_shape=[jax.ShapeDtypeStruct((s, d), F32), jax.ShapeDtypeStruct((s, d), BF16)],
        compiler_params=_params("parallel"),
        name="add_rmsnorm",
    )(x, y, g_post.reshape(1, d), g_pre.reshape(1, d))


def _mm_kernel(a_ref, b_ref, o_ref):
    o_ref[...] = jnp.dot(a_ref[...], b_ref[...], preferred_element_type=F32).astype(o_ref.dtype)


def matmul(a, b, out_dtype, tm=512, tn=512):
    m, k = a.shape
    _, n = b.shape
    return pl.pallas_call(
        _mm_kernel,
        grid=(m // tm, n // tn),
        in_specs=[pl.BlockSpec((tm, k), lambda i, j: (i, 0)),
                  pl.BlockSpec((k, tn), lambda i, j: (0, j))],
        out_specs=pl.BlockSpec((tm, tn), lambda i, j: (i, j)),
        out_shape=jax.ShapeDtypeStruct((m, n), out_dtype),
        compiler_params=_params("parallel", "parallel"),
        name="matmul",
    )(a, b)


def _mm2_kernel(a1_ref, a2_ref, b1_ref, b2_ref, o_ref):
    acc = jnp.dot(a1_ref[...], b1_ref[...], preferred_element_type=F32)
    acc += jnp.dot(a2_ref[...], b2_ref[...], preferred_element_type=F32)
    o_ref[...] = acc.astype(o_ref.dtype)


def matmul_cat(a1, a2, b, out_dtype, tm=512, tn=512):
    m, k1 = a1.shape
    _, k2 = a2.shape
    assert k1 == k2 and b.shape[0] == k1 + k2
    n = b.shape[1]
    return pl.pallas_call(
        _mm2_kernel,
        grid=(m // tm, n // tn),
        in_specs=[pl.BlockSpec((tm, k1), lambda i, j: (i, 0)),
                  pl.BlockSpec((tm, k2), lambda i, j: (i, 0)),
                  pl.BlockSpec((k1, tn), lambda i, j: (0, j)),
                  pl.BlockSpec((k2, tn), lambda i, j: (1, j))],
        out_specs=pl.BlockSpec((tm, tn), lambda i, j: (i, j)),
        out_shape=jax.ShapeDtypeStruct((m, n), out_dtype),
        compiler_params=_params("parallel", "parallel"),
        name="matmul_cat",
    )(a1, a2, b, b)


def _shift_rows(x, prev, d):
    y = pltpu.roll(x, d, 0)
    row = lax.broadcasted_iota(jnp.int32, x.shape, 0)
    for r in range(d):
        y = jnp.where(row == r, prev[SUBLANES - d + r:SUBLANES - d + r + 1, :], y)
    return y


def _ffn_up_kernel(h_ref, wg_ref, wv_ref, cg_ref, cv_ref, o_ref, hg_ref, hv_ref):
    i = pl.program_id(1)

    @pl.when(i == 0)
    def _():
        hg_ref[...] = jnp.zeros_like(hg_ref)
        hv_ref[...] = jnp.zeros_like(hv_ref)

    h = h_ref[...]

    def conv(w_ref, c_ref, halo_ref):
        raw = jnp.dot(h, w_ref[...], preferred_element_type=F32)
        prev = halo_ref[...]
        c = c_ref[...]
        out = (c[2:3, :] * raw + c[1:2, :] * _shift_rows(raw, prev, 1)
               + c[0:1, :] * _shift_rows(raw, prev, 2))
        halo_ref[...] = raw[raw.shape[0] - SUBLANES:, :]
        return out

    gate = conv(wg_ref, cg_ref, hg_ref)
    val = conv(wv_ref, cv_ref, hv_ref)
    o_ref[...] = (jax.nn.gelu(gate, approximate=True) * val).astype(o_ref.dtype)


def ffn_up(h, w_up, conv_w, tm=512, tn=512):
    s, d = h.shape
    f = w_up.shape[1] // 2
    nj = f // tn
    return pl.pallas_call(
        _ffn_up_kernel,
        grid=(nj, s // tm),
        in_specs=[pl.BlockSpec((tm, d), lambda j, i: (i, 0)),
                  pl.BlockSpec((d, tn), lambda j, i: (0, j)),
                  pl.BlockSpec((d, tn), lambda j, i: (0, j + nj)),
                  pl.BlockSpec((FFN_CONV, tn), lambda j, i: (0, j)),
                  pl.BlockSpec((FFN_CONV, tn), lambda j, i: (0, j + nj))],
        out_specs=pl.BlockSpec((tm, tn), lambda j, i: (i, j)),
        out_shape=jax.ShapeDtypeStruct((s, f), BF16),
        scratch_shapes=[pltpu.VMEM((SUBLANES, tn), F32), pltpu.VMEM((SUBLANES, tn), F32)],
        compiler_params=_params("parallel", "arbitrary"),
        name="ffn_up",
    )(h, w_up, w_up, conv_w, conv_w)


def _dot_nt(a, b):
    return lax.dot_general(a, b, (((1,), (1,)), ((), ())), preferred_element_type=F32)


def _moba_kernel(slope_ref, q_ref, k_ref, v_ref, o_ref, kbar_ref, vt_ref,
                 s0_buf, s1_buf, p0_buf, p1_buf, *, n_blk):
    hd = pl.program_id(0)
    i = pl.program_id(1)
    bs = MOBA_BLOCK
    slope = slope_ref[hd]

    def pair_rows(ref, t):
        return ref[pl.ds(pl.multiple_of(t * (2 * bs), 2 * bs), 2 * bs), :]

    @pl.when(i == 0)
    def _():
        def kmean(j, c):
            kb = k_ref[pl.ds(pl.multiple_of(j * bs, bs), bs), :].astype(F32)
            kbar_ref[pl.ds(j, 1), :] = jnp.mean(kb, axis=0, keepdims=True)
            return c
        lax.fori_loop(0, n_blk, kmean, 0)

        def vtrans(t, c):
            vt_ref[t] = pair_rows(v_ref, t).astype(F32).T.astype(BF16)
            return c
        lax.fori_loop(0, n_blk // 2, vtrans, 0)

    q = q_ref[...]
    kbar = kbar_ref[...]
    kb_hi = kbar.astype(BF16)
    kb_lo = (kbar - kb_hi.astype(F32)).astype(BF16)
    gate = _dot_nt(kb_hi, q) + _dot_nt(kb_lo, q)
    blk = lax.broadcasted_iota(jnp.int32, gate.shape, 0)
    gate = jnp.where(blk < i, gate, -jnp.inf)
    sel = []
    for _ in range(MOBA_TOPK):
        m = jnp.max(gate, axis=0, keepdims=True)
        cand = (gate == m) & (gate > -jnp.inf)
        idx = jnp.min(jnp.where(cand, blk, n_blk), axis=0, keepdims=True)
        sel.append(idx)
        gate = jnp.where(blk == idx, -jnp.inf, gate)

    slope2 = slope * LOG2E
    key = lax.broadcasted_iota(jnp.int32, (bs, bs), 0)
    qry = lax.broadcasted_iota(jnp.int32, (bs, bs), 1)
    cb_pair = slope2 * lax.broadcasted_iota(jnp.int32, (2 * bs, bs), 0).astype(F32)

    i0 = pl.multiple_of(i * bs, bs)
    s = _dot_nt(k_ref[pl.ds(i0, bs), :], q) + slope2 * key.astype(F32)
    s = jnp.where(key <= qry, s, NEG_BIG)
    m0 = jnp.max(s, axis=0, keepdims=True)
    p = jnp.exp2(s - m0)
    l0 = jnp.sum(p, axis=0, keepdims=True)
    acc0 = jnp.dot(v_ref[pl.ds(i0, bs), :].astype(F32).T.astype(BF16), p.astype(BF16),
                   preferred_element_type=F32)

    def qbias(j):
        picked = (sel[0] == j) | (sel[1] == j) | (sel[2] == j)
        return jnp.where(picked, 0.0, NEG_BIG)

    n_it = (i + 1) // 2
    t_last = jnp.maximum(n_it - 1, 0)

    s_buf = (s0_buf, s1_buf)
    p_buf = (p0_buf, p1_buf)
    s_buf[0][...] = _dot_nt(pair_rows(k_ref, 0), q)
    p_buf[1][...] = jnp.zeros((2 * bs, bs), BF16)

    def phase(t, carry, cur):
        m_i, l_i, acc = carry
        s_buf[1 - cur][...] = _dot_nt(pair_rows(k_ref, jnp.minimum(t + 1, t_last)), q)
        pv = jnp.dot(vt_ref[jnp.clip(t - 1, 0, t_last)], p_buf[1 - cur][...],
                     preferred_element_type=F32)
        ja = 2 * t
        s = s_buf[cur][...] + cb_pair
        sa, sb = s[:bs], s[bs:]
        off = slope2 * ((ja - i) * bs).astype(F32)
        qa, qb = qbias(ja) + off, qbias(ja + 1) + off
        m_new = jnp.maximum(m_i, jnp.maximum(jnp.max(sa, axis=0, keepdims=True) + qa,
                                             jnp.max(sb, axis=0, keepdims=True) + qb))
        pa = jnp.exp2(sa + (qa - m_new))
        pb = jnp.exp2(sb + (qb - m_new))
        alpha = jnp.exp2(m_i - m_new)
        l_new = (alpha * l_i + jnp.sum(pa, axis=0, keepdims=True)
                 + jnp.sum(pb, axis=0, keepdims=True))
        p_buf[cur][...] = jnp.concatenate([pa, pb], axis=0).astype(BF16)
        return m_new, l_new, alpha * (acc + pv)

    def body(u, carry):
        return phase(2 * u + 1, phase(2 * u, carry, 0), 1)

    n_trip = (n_it + 1) // 2
    _, l_f, acc_f = lax.fori_loop(0, n_trip, body, (m0, l0, acc0))
    acc_f = acc_f + jnp.dot(vt_ref[jnp.clip(2 * n_trip - 1, 0, t_last)], p_buf[1][...],
                            preferred_element_type=F32)
    o_ref[...] = (acc_f / l_f).T.astype(o_ref.dtype)


def moba_attention(proj, n_heads):
    s = proj.shape[0]
    dh, bs = MOBA_HEAD_DIM, MOBA_BLOCK
    n_blk = s // bs
    slopes = 2.0 ** (-8.0 * jnp.arange(1, n_heads + 1, dtype=F32) / n_heads)
    grid_spec = pltpu.PrefetchScalarGridSpec(
        num_scalar_prefetch=1,
        grid=(n_heads, n_blk),
        in_specs=[pl.BlockSpec((bs, dh), lambda h, i, sl: (i, h)),
                  pl.BlockSpec((s, dh), lambda h, i, sl: (0, n_heads + h)),
                  pl.BlockSpec((s, dh), lambda h, i, sl: (0, 2 * n_heads + h))],
        out_specs=pl.BlockSpec((bs, dh), lambda h, i, sl: (i, h)),
        scratch_shapes=[pltpu.VMEM((n_blk, dh), F32),
                        pltpu.VMEM((n_blk // 2, dh, 2 * bs), BF16),
                        pltpu.VMEM((2 * bs, bs), F32), pltpu.VMEM((2 * bs, bs), F32),
                        pltpu.VMEM((2 * bs, bs), BF16), pltpu.VMEM((2 * bs, bs), BF16)],
    )
    return pl.pallas_call(
        functools.partial(_moba_kernel, n_blk=n_blk),
        grid_spec=grid_spec,
        out_shape=jax.ShapeDtypeStruct((s, n_heads * dh), BF16),
        compiler_params=_params("parallel", "arbitrary"),
        name="moba",
    )(slopes, proj, proj, proj)


def _softplus(z):
    return jnp.maximum(z, 0.0) + jnp.log1p(jnp.exp(-jnp.abs(z)))


def _sb_kernel(q_ref, k_ref, v_ref, o_ref, z0_buf, z1_buf, w0_buf, w1_buf, *, tb):
    i = pl.program_id(1)
    q = q_ref[...]
    row = lax.broadcasted_iota(jnp.int32, (tb, tb), 0)
    col = lax.broadcasted_iota(jnp.int32, (tb, tb), 1)
    tri = jnp.where(row >= col, 1.0, 0.0).astype(BF16)
    past = col < row

    def weights(z2, later, masked):
        nz = -z2
        lk = jnp.minimum(nz, 0.0) - jnp.log2(1.0 + jnp.exp2(jnp.minimum(z2, nz)))
        if masked:
            lk = jnp.where(past, lk, 0.0)
        c = jnp.dot(lk.astype(BF16), tri, preferred_element_type=F32)
        w = jnp.exp2(z2 + c + later)
        if masked:
            w = jnp.where(past, w, 0.0)
        return w.astype(BF16), jnp.sum(lk, axis=-1, keepdims=True)

    i0 = pl.multiple_of(i * tb, tb)
    w0, later0 = weights(_dot_nt(q, k_ref[pl.ds(i0, tb), :]), jnp.zeros((tb, 1), F32), True)
    acc0 = jnp.dot(w0, v_ref[pl.ds(i0, tb), :], preferred_element_type=F32)

    n_trip = i // 4
    t_last = jnp.maximum(2 * n_trip - 1, 0)
    z_buf = (z0_buf, z1_buf)
    w_buf = (w0_buf, w1_buf)

    def pair_rows(ref, t):
        j0 = jnp.maximum(i - 2 - 2 * t, 0) * tb
        return ref[pl.ds(pl.multiple_of(j0, tb), 2 * tb), :]

    z_buf[0][...] = _dot_nt(q, pair_rows(k_ref, 0))
    w_buf[1][...] = jnp.zeros((tb, 2 * tb), BF16)

    def pair(t, carry, cur):
        acc, later = carry
        z_buf[1 - cur][...] = _dot_nt(q, pair_rows(k_ref, jnp.minimum(t + 1, t_last)))
        acc = acc + jnp.dot(w_buf[1 - cur][...], pair_rows(v_ref, jnp.maximum(t - 1, 0)),
                            preferred_element_type=F32)
        z2 = z_buf[cur][...]
        w_hi, tot_hi = weights(z2[:, tb:], later, False)
        later = later + tot_hi
        w_lo, tot_lo = weights(z2[:, :tb], later, False)
        w_buf[cur][...] = jnp.concatenate([w_lo, w_hi], axis=1)
        return acc, later + tot_lo

    def trip(u, carry):
        return pair(2 * u + 1, pair(2 * u, carry, 0), 1)

    acc, later = lax.fori_loop(0, n_trip, trip, (acc0, later0))
    acc = acc + jnp.dot(w_buf[1][...], pair_rows(v_ref, t_last), preferred_element_type=F32)

    n_left = i - 4 * n_trip

    def single(t, carry):
        acc, later = carry
        j0 = pl.multiple_of((n_left - 1 - t) * tb, tb)
        w, tot = weights(_dot_nt(q, k_ref[pl.ds(j0, tb), :]), later, False)
        return acc + jnp.dot(w, v_ref[pl.ds(j0, tb), :], preferred_element_type=F32), later + tot

    acc, _ = lax.fori_loop(0, n_left, single, (acc, later))
    o_ref[...] = acc.astype(o_ref.dtype)


def sb_attention(proj, n_heads, col0, tb=256):
    s = proj.shape[0]
    dh = SB_HEAD_DIM
    c0 = col0 // dh
    return pl.pallas_call(
        functools.partial(_sb_kernel, tb=tb),
        grid=(n_heads, s // tb),
        in_specs=[pl.BlockSpec((tb, dh), lambda h, i: (i, c0 + h)),
                  pl.BlockSpec((s, dh), lambda h, i: (0, c0 + n_heads + h)),
                  pl.BlockSpec((s, dh), lambda h, i: (0, c0 + 2 * n_heads + h))],
        out_specs=pl.BlockSpec((tb, dh), lambda h, i: (i, h)),
        out_shape=jax.ShapeDtypeStruct((s, n_heads * dh), BF16),
        scratch_shapes=[pltpu.VMEM((tb, 2 * tb), F32), pltpu.VMEM((tb, 2 * tb), F32),
                        pltpu.VMEM((tb, 2 * tb), BF16), pltpu.VMEM((tb, 2 * tb), BF16)],
        compiler_params=_params("parallel", "arbitrary"),
        name="stickbreak",
    )(proj, proj, proj)


RWKV_W = 2048
LORA_W = 640
HEAD_GROUP = 256
RWKV_CHUNK = 64


def _sigmoid(x):
    return 1.0 / (1.0 + jnp.exp(-x))


def _head_ones(n):
    r = lax.broadcasted_iota(jnp.int32, (n, n), 0) // RWKV_HEAD
    c = lax.broadcasted_iota(jnp.int32, (n, n), 1) // RWKV_HEAD
    return jnp.where(r == c, 1.0, 0.0).astype(BF16)


def _head_sum(x):
    e = _head_ones(HEAD_GROUP)
    hi = x.astype(BF16)
    lo = (x - hi.astype(F32)).astype(BF16)
    outs = []
    for c in range(x.shape[1] // HEAD_GROUP):
        sl = slice(c * HEAD_GROUP, (c + 1) * HEAD_GROUP)
        outs.append(jnp.dot(hi[:, sl], e, preferred_element_type=F32)
                    + jnp.dot(lo[:, sl], e, preferred_element_type=F32))
    return jnp.concatenate(outs, axis=-1)


def _token_shift(x_ref, halo_ref, mu_ref, first):
    x = x_ref[...]
    prev = jnp.where(first, 0.0, halo_ref[...])
    return x + (_shift_rows(x, prev, 1) - x) * mu_ref[...]


def _rwkv_prep_kernel(*refs, vres):
    (r_ref, k_ref, v_ref, lo_ref, rh_ref, kh_ref, vh_ref, loh_ref,
     mur_ref, muk_ref, muv_ref, mulo_ref,
     w0_ref, w2_ref, a0_ref, a2_ref, v0_ref, v2_ref, g2_ref, kk_ref, ka_ref, rk_ref) = refs[:22]
    rest = refs[22:]
    if vres:
        vf_ref, rest = rest[0], rest[1:]
    ro_ref, lw_ref, ko_ref, vo_ref, na_ref, bb_ref, g_ref, bo_ref = rest
    first = pl.program_id(0) == 0
    r = _token_shift(r_ref, rh_ref, mur_ref, first)
    k = _token_shift(k_ref, kh_ref, muk_ref, first)
    v = _token_shift(v_ref, vh_ref, muv_ref, first)
    lo = _token_shift(lo_ref, loh_ref, mulo_ref, first)
    w_lo, a_lo, v_lo, g_lo = lo[:, 0:128], lo[:, 128:256], lo[:, 256:384], lo[:, 384:640]

    def lora(x, w_ref):
        return jnp.dot(x.astype(BF16), w_ref[...], preferred_element_type=F32)

    w_log = -_softplus(-(w0_ref[...] + lora(jnp.tanh(w_lo), w2_ref))) - 0.5
    lw_ref[...] = -jnp.exp(w_log)
    if vres:
        v = v + (vf_ref[...] - v) * _sigmoid(v0_ref[...] + lora(v_lo, v2_ref))
    a = _sigmoid(a0_ref[...] + lora(a_lo, a2_ref))
    g_ref[...] = lora(_sigmoid(g_lo), g2_ref)
    kk = k * kk_ref[...]
    kk = kk / jnp.maximum(jnp.sqrt(_head_sum(kk * kk)), 1e-12)
    k2 = k * (1.0 + (a - 1.0) * ka_ref[...])
    ro_ref[...] = r
    ko_ref[...] = k2
    vo_ref[...] = v
    na_ref[...] = -kk
    bb_ref[...] = kk * a
    bo_ref[...] = _head_sum(r * k2 * rk_ref[...]) * v


def rwkv_prep(proj, col0, mu, w0, w2p, a0, a2p, v0, v2p, g2, k_k, k_a, r_k, v_first, tm=128):
    s = proj.shape[0]
    w = RWKV_W
    vres = v_first is not None
    c_r, c_lo = col0 // w, (col0 + 3 * w) // LORA_W
    hb = tm // SUBLANES

    def main(width, cidx):
        return pl.BlockSpec((tm, width), lambda i: (i, cidx))

    def halo(width, cidx):
        return pl.BlockSpec((SUBLANES, width), lambda i: (jnp.maximum(i * hb - 1, 0), cidx))

    def vec(width, cidx=0):
        return pl.BlockSpec((1, width), lambda i: (0, cidx))

    def full(shape):
        return pl.BlockSpec(shape, lambda i: (0, 0))

    mu2 = mu.reshape(1, -1)
    r1 = lambda t: t.reshape(1, -1)
    in_specs = [main(w, c_r), main(w, c_r + 1), main(w, c_r + 2), main(LORA_W, c_lo),
                halo(w, c_r), halo(w, c_r + 1), halo(w, c_r + 2), halo(LORA_W, c_lo),
                vec(w, c_r), vec(w, c_r + 1), vec(w, c_r + 2), vec(LORA_W, c_lo),
                vec(w), full((128, w)), vec(w), full((128, w)), vec(w), full((128, w)),
                full((256, w)), vec(w), vec(w), vec(w)]
    args = [proj] * 8 + [mu2] * 4 + [r1(w0), w2p, r1(a0), a2p, r1(v0), v2p, g2, r1(k_k), r1(k_a), r1(r_k)]
    if vres:
        in_specs.append(main(w, 0))
        args.append(v_first)
    out = pl.BlockSpec((tm, w), lambda i: (i, 0))
    return pl.pallas_call(
        functools.partial(_rwkv_prep_kernel, vres=vres),
        grid=(s // tm,),
        in_specs=in_specs,
        out_specs=[out] * 8,
        out_shape=[jax.ShapeDtypeStruct((s, w), F32)] * 8,
        compiler_params=_params("parallel"),
        name="rwkv_prep",
    )(*args)


def _cumsum_rows(x):
    n = x.shape[0]
    row = lax.broadcasted_iota(jnp.int32, x.shape, 0)
    d = 1
    while d < n:
        x = x + jnp.where(row >= d, pltpu.roll(x, d, 0), 0.0)
        d *= 2
    return x


def _rwkv_chunk(r, lw, k, v, a, b, state):
    L, n2 = r.shape
    nh = n2 // RWKV_HEAD
    J = nh * L
    bd = lambda x: x.astype(BF16)
    mm = lambda x, y: jnp.dot(bd(x), bd(y), preferred_element_type=F32)
    lg = _cumsum_rows(lw)
    tot = lg[L - 1:L, :]
    e_neg = jnp.exp(-lg)
    e_hat = jnp.exp(tot - lg)
    a_t = a * jnp.exp(lg - lw)
    r_t = r * jnp.exp(lg)
    b_i, k_i = b * e_neg, k * e_neg
    b_hat, k_hat = b * e_hat, k * e_hat

    head = lax.broadcasted_iota(jnp.int32, (L, n2), 1) // RWKV_HEAD

    def stack(x):
        return jnp.concatenate([jnp.where(head == h, x, 0.0) for h in range(nh)], axis=0)

    def fold(x):
        out = x[:L]
        for h in range(1, nh):
            out = out + x[h * L:(h + 1) * L]
        return out

    row = lax.broadcasted_iota(jnp.int32, (J, J), 0)
    col = lax.broadcasted_iota(jnp.int32, (J, J), 1)
    same = (row // L) == (col // L)
    strict = same & (col < row)
    incl = same & (col <= row)
    eye = jnp.where(row == col, 1.0, 0.0)

    x_a, x_r = stack(a_t), stack(r_t)
    ybk = jnp.concatenate([b_i] * nh + [k_i] * nh, axis=0)
    pa = _dot_nt(bd(x_a), bd(ybk))
    pr = _dot_nt(bd(x_r), bd(ybk))
    yield
    m_ab = jnp.where(strict, pa[:, :J], 0.0)
    m_ak = jnp.where(strict, pa[:, J:], 0.0)
    m_rb = jnp.where(incl, pr[:, :J], 0.0)
    m_rk = jnp.where(incl, pr[:, J:], 0.0)

    v_st = stack(v)
    w1 = mm(m_ak, v_st)
    t_inv, pw = eye + m_ab, m_ab
    for _ in range(int(math.log2(L)) - 1):
        pw = mm(pw, pw)
        yield
        t_inv = mm(t_inv, eye + pw)

    yield
    tw = mm(t_inv, jnp.concatenate([x_a, w1], axis=1))
    yield
    ta, w0 = fold(tw[:, :n2]), fold(tw[:, n2:])
    gs = _dot_nt(bd(jnp.concatenate([ta, r_t], axis=0)), bd(state))
    lhs_t = jnp.concatenate([jnp.concatenate([ta, jnp.zeros_like(ta)], axis=0).T,
                             jnp.concatenate([w0, v], axis=0).T], axis=0)
    pq = mm(lhs_t, jnp.concatenate([b_hat, k_hat], axis=0))
    yield
    u = gs[:L] + w0
    yj = mm(jnp.concatenate([m_rb, m_rk], axis=1), jnp.concatenate([stack(u), v_st], axis=0))
    y = gs[L:] + fold(yj)
    yield
    ch_r = lax.broadcasted_iota(jnp.int32, (n2, n2), 0)
    ch_c = lax.broadcasted_iota(jnp.int32, (n2, n2), 1)
    head_blk = (ch_r // RWKV_HEAD) == (ch_c // RWKV_HEAD)
    p_mat = jnp.where(head_blk, pq[:n2], 0.0) + jnp.where(ch_r == ch_c, jnp.exp(tot), 0.0)
    q_mat = jnp.where(head_blk, pq[n2:], 0.0)
    return y, mm(state, p_mat) + q_mat


def _run_lockstep(chains):
    results = [None] * len(chains)
    live = list(range(len(chains)))
    while live:
        for g in list(live):
            try:
                next(chains[g])
            except StopIteration as done:
                results[g] = done.value
                live.remove(g)
    return results


def _rwkv_rec_kernel(r_ref, lw_ref, k_ref, v_ref, a_ref, b_ref, y_ref, st_ref, *, n_groups, n_chunks):
    gw = st_ref.shape[-1]

    @pl.when(pl.program_id(1) == 0)
    def _():
        st_ref[...] = jnp.zeros_like(st_ref)

    def body(c, carry):
        rows = pl.ds(pl.multiple_of(c * RWKV_CHUNK, RWKV_CHUNK), RWKV_CHUNK)
        in_refs = (r_ref, lw_ref, k_ref, v_ref, a_ref, b_ref)
        chains = [_rwkv_chunk(*[ref[rows, p * gw:(p + 1) * gw] for ref in in_refs], st_ref[p])
                  for p in range(n_groups)]
        for p, (y, st) in enumerate(_run_lockstep(chains)):
            y_ref[rows, p * gw:(p + 1) * gw] = y
            st_ref[p] = st
        return carry

    lax.fori_loop(0, n_chunks, body, 0)


def rwkv_recurrence(r, lw, k, v, a, b, tt=512, n_groups=2, heads_per_group=4):
    s, w = r.shape
    gw = heads_per_group * RWKV_HEAD
    wb = n_groups * gw
    blk = pl.BlockSpec((tt, wb), lambda h, t: (t, h))
    return pl.pallas_call(
        functools.partial(_rwkv_rec_kernel, n_groups=n_groups, n_chunks=tt // RWKV_CHUNK),
        grid=(w // wb, s // tt),
        in_specs=[blk] * 6,
        out_specs=blk,
        out_shape=jax.ShapeDtypeStruct((s, w), F32),
        scratch_shapes=[pltpu.VMEM((n_groups, gw, gw), F32)],
        compiler_params=_params("arbitrary", "arbitrary"),
        name="rwkv_rec",
    )(r, lw, k, v, a, b)


def _rwkv_post_kernel(y_ref, g_ref, bo_ref, lw_ref, lb_ref, o_ref):
    y = y_ref[...]
    inv_n = 1.0 / RWKV_HEAD
    mu = _head_sum(y) * inv_n
    yc = y - mu
    var = _head_sum(yc * yc) * inv_n
    yn = yc * lax.rsqrt(var + GN_EPS) * lw_ref[...] + lb_ref[...]
    o_ref[...] = ((yn + bo_ref[...]) * g_ref[...]).astype(o_ref.dtype)


def rwkv_post(y, g, bonus, lnx_w, lnx_b, tm=256):
    s, w = y.shape
    row = pl.BlockSpec((tm, w), lambda i: (i, 0))
    vec = pl.BlockSpec((1, w), lambda i: (0, 0))
    return pl.pallas_call(
        _rwkv_post_kernel,
        grid=(s // tm,),
        in_specs=[row, row, row, vec, vec],
        out_specs=row,
        out_shape=jax.ShapeDtypeStruct((s, w), BF16),
        compiler_params=_params("parallel"),
        name="rwkv_post",
    )(y, g, bonus, lnx_w.reshape(1, w), lnx_b.reshape(1, w))


def _expm1(y):
    u = jnp.exp(y)
    near = (u - 1.0) * y / jnp.log(u)
    return jnp.where(u == 1.0, y, jnp.where(y < -0.5, u - 1.0, near))


def _lru_kernel(gate_ref, x_ref, xh_ref, cw_ref, cb_ref, gw_ref, gb_ref, lam_ref, o_ref, h_ref, *, nb):
    t = pl.program_id(1)

    @pl.when(t == 0)
    def _():
        h_ref[...] = jnp.zeros_like(h_ref)

    x = x_ref[...]
    prev = jnp.where(t == 0, 0.0, xh_ref[...])
    cw = cw_ref[...]
    xb = cw[3:4, :] * x + cb_ref[...]
    for d in range(1, LRU_CONV):
        xb = xb + cw[3 - d:4 - d, :] * _shift_rows(x, prev, d)
    xb16 = xb.astype(BF16)
    g0, g1 = [], []
    for n in range(nb):
        xs = xb16[:, n * LANES:(n + 1) * LANES]
        g0.append(jnp.dot(xs, gw_ref[0, n], preferred_element_type=F32))
        g1.append(jnp.dot(xs, gw_ref[1, n], preferred_element_type=F32))
    gb = gb_ref[...]
    g0 = jnp.concatenate(g0, axis=-1) + gb[0:1, :]
    g1 = jnp.concatenate(g1, axis=-1) + gb[1:2, :]
    log_a = -LRU_C * _sigmoid(g0) * _softplus(-lam_ref[...])
    a = jnp.exp(log_a)
    b = jnp.sqrt(-_expm1(2.0 * log_a)) * _sigmoid(g1) * xb
    n = x.shape[0]
    row = lax.broadcasted_iota(jnp.int32, x.shape, 0)
    d = 1
    while d < n:
        keep = row >= d
        a_s = jnp.where(keep, pltpu.roll(a, d, 0), 1.0)
        b_s = jnp.where(keep, pltpu.roll(b, d, 0), 0.0)
        b = a * b_s + b
        a = a * a_s
        d *= 2
    h = b + a * h_ref[SUBLANES - 1:SUBLANES, :]
    h_ref[...] = h[n - SUBLANES:, :]
    o_ref[...] = (h * jax.nn.gelu(gate_ref[...], approximate=True)).astype(o_ref.dtype)


def rg_lru(proj, conv_w, conv_b, gate_w, gate_b, lam, tm=256, tc=512):
    s = proj.shape[0]
    w = conv_w.shape[1]
    nj = w // tc
    nb = tc // LANES
    hb = tm // SUBLANES
    return pl.pallas_call(
        functools.partial(_lru_kernel, nb=nb),
        grid=(nj, s // tm),
        in_specs=[pl.BlockSpec((tm, tc), lambda j, t: (t, j)),
                  pl.BlockSpec((tm, tc), lambda j, t: (t, nj + j)),
                  pl.BlockSpec((SUBLANES, tc), lambda j, t: (jnp.maximum(t * hb - 1, 0), nj + j)),
                  pl.BlockSpec((LRU_CONV, tc), lambda j, t: (0, j)),
                  pl.BlockSpec((1, tc), lambda j, t: (0, j)),
                  pl.BlockSpec((2, nb, LANES, LANES), lambda j, t: (0, j, 0, 0)),
                  pl.BlockSpec((2, tc), lambda j, t: (0, j)),
                  pl.BlockSpec((1, tc), lambda j, t: (0, j))],
        out_specs=pl.BlockSpec((tm, tc), lambda j, t: (t, j)),
        out_shape=jax.ShapeDtypeStruct((s, w), BF16),
        scratch_shapes=[pltpu.VMEM((SUBLANES, tc), F32)],
        compiler_params=_params("arbitrary", "arbitrary"),
        name="rg_lru",
    )(proj, proj, proj, conv_w, conv_b.reshape(1, w), gate_w.astype(BF16), gate_b, lam.reshape(1, w))


def even_mixer(h, w_in, w_out):
    half = w_out.shape[0] // 2
    n_moba = half // MOBA_HEAD_DIM
    n_sb = half // SB_HEAD_DIM
    col = jnp.arange(w_in.shape[1])
    qscale = jnp.where(col < half, MOBA_HEAD_DIM ** -0.5 * LOG2E,
                       jnp.where((col >= 3 * half) & (col < 4 * half), SB_HEAD_DIM ** -0.5 * LOG2E, 1.0))
    proj = matmul(h, (w_in * qscale.astype(F32)).astype(BF16), BF16)
    oa = moba_attention(proj, n_moba)
    ob = sb_attention(proj, n_sb, 3 * n_moba * MOBA_HEAD_DIM)
    return matmul_cat(oa, ob, w_out.astype(BF16), F32)


def _pad_cols(t, width):
    return jnp.pad(t, [(0, 0)] * (t.ndim - 1) + [(0, width - t.shape[-1])])


def _odd_layout(t, n_c, vres):
    w = RWKV_W
    sizes = [96, 96] + ([64] if vres else []) + [256]
    segs, c = [], 3 * w
    for sz in sizes:
        segs.append(_pad_cols(t[..., c:c + sz], LANES if sz < LANES else sz))
        c += sz
    if not vres:
        segs.insert(2, jnp.zeros(t.shape[:-1] + (LANES,), t.dtype))
    assert c == n_c
    return jnp.concatenate([t[..., n_c:], t[..., :3 * w]] + segs, axis=-1)


def _pad_rows(t, rows):
    return jnp.pad(t, ((0, rows - t.shape[0]), (0, 0)))


def odd_mixer(h, v_first, p):
    w = RWKV_W
    vres = "v2" in p
    n_c = p["shift_mu"].shape[0]
    w_in = _odd_layout(p["w_in"], n_c, vres).astype(BF16)
    lru_w = p["w_in"].shape[1] - n_c
    mu = _odd_layout(jnp.concatenate([p["shift_mu"], jnp.zeros((lru_w,), F32)]), n_c, vres)
    proj = matmul(h, w_in, F32, tn=LORA_W)
    zeros_w = jnp.zeros((w,), F32)
    r, lw, k, v, na, bb, g, bonus = rwkv_prep(
        proj, lru_w, mu, p["w0"], _pad_rows(p["w2"], LANES).astype(BF16),
        p["a0"], _pad_rows(p["a2"], LANES).astype(BF16),
        p["v0"] if vres else zeros_w,
        _pad_rows(p["v2"], LANES).astype(BF16) if vres else jnp.zeros((LANES, w), BF16),
        p["g2"].astype(BF16), p["k_k"], p["k_a"], p["r_k"].reshape(-1), v_first)
    if v_first is None:
        v_first = v
    y = rwkv_recurrence(r, lw, k, v, na, bb)
    c_out = rwkv_post(y, g, bonus, p["lnx_w"], p["lnx_b"])
    d_out = rg_lru(proj, p["conv_w"], p["conv_b"], p["gate_w"], p["gate_b"], p["lru_lambda"])
    return matmul_cat(c_out, d_out, p["w_out"].astype(BF16), F32), v_first


_ODD_NAMES = ("w_in", "w_out", "shift_mu", "w0", "w2", "a0", "a2", "g2", "k_k", "k_a", "r_k",
              "lnx_w", "lnx_b", "conv_w", "conv_b", "gate_w", "gate_b", "lru_lambda")


def _forward(x, layers):
    x2 = x.reshape(x.shape[-2], x.shape[-1])
    h = rmsnorm(x2, layers[0]["norms"][0])
    v_first = None
    for li, p in enumerate(layers):
        g = p["norms"]
        if "shift_mu" in p:
            mix, v_first = odd_mixer(h, v_first, p)
        else:
            mix = even_mixer(h, p["w_in"], p["w_out"])
        x2, h = add_rmsnorm(x2, mix, g[1], g[2])
        act = ffn_up(h, p["ffn_up"].astype(BF16), p["ffn_conv"])
        y = matmul(act, p["ffn_down"].astype(BF16), F32)
        g_next = layers[li + 1]["norms"][0] if li + 1 < len(layers) else g[3]
        x2, h = add_rmsnorm(x2, y, g[3], g_next)
    return x2.reshape(x.shape)


def kernel(x, l0_norms, l0_w_in, l0_w_out, l0_ffn_up, l0_ffn_conv, l0_ffn_down, l1_norms, l1_w_in, l1_w_out, l1_shift_mu, l1_w0, l1_w2, l1_a0, l1_a2, l1_g2, l1_k_k, l1_k_a, l1_r_k, l1_lnx_w, l1_lnx_b, l1_conv_w, l1_conv_b, l1_gate_w, l1_gate_b, l1_lru_lambda, l1_ffn_up, l1_ffn_conv, l1_ffn_down, l2_norms, l2_w_in, l2_w_out, l2_ffn_up, l2_ffn_conv, l2_ffn_down, l3_norms, l3_w_in, l3_w_out, l3_shift_mu, l3_w0, l3_w2, l3_a0, l3_a2, l3_v0, l3_v2, l3_g2, l3_k_k, l3_k_a, l3_r_k, l3_lnx_w, l3_lnx_b, l3_conv_w, l3_conv_b, l3_gate_w, l3_gate_b, l3_lru_lambda, l3_ffn_up, l3_ffn_conv, l3_ffn_down):
    a = locals()
    layers = []
    for li in range(4):
        pre = "l%d_" % li
        layers.append({k[len(pre):]: v for k, v in a.items() if k.startswith(pre)})
    return _forward(x, layers)
```

```python
import functools
import math

import jax
import jax.numpy as jnp
from jax import lax
from jax.experimental import pallas as pl
from jax.experimental.pallas import tpu as pltpu

F32 = jnp.float32
BF16 = jnp.bfloat16

V7X_VMEM_BYTES = 64 * 1024 * 1024
VMEM_LIMIT = 56 * 1024 * 1024
LANES = 128
SUBLANES = 8

RMS_EPS = 1e-6
GN_EPS = 64e-5
MOBA_HEAD_DIM = 128
MOBA_BLOCK = 256
MOBA_TOPK = 3
MOBA_PHASES = 4
SB_HEAD_DIM = 256
SB_PAIRS = 4
RWKV_HEAD = 64
LRU_BLOCKS = 16
LRU_CONV = 4
LRU_C = 8.0
FFN_CONV = 3
NEG_BIG = -1e30
LOG2E = 1.4426950408889634
LN2 = 0.6931471805599453


def _params(*sem):
    return pltpu.CompilerParams(dimension_semantics=sem, vmem_limit_bytes=VMEM_LIMIT)


def _rms(x, g):
    return x * lax.rsqrt(jnp.mean(x * x, axis=-1, keepdims=True) + RMS_EPS) * g


def _rmsnorm_kernel(x_ref, g_ref, o_ref):
    o_ref[...] = _rms(x_ref[...], g_ref[...]).astype(o_ref.dtype)


def rmsnorm(x, g, tm=256):
    s, d = x.shape
    return pl.pallas_call(
        _rmsnorm_kernel,
        grid=(s // tm,),
        in_specs=[pl.BlockSpec((tm, d), lambda i: (i, 0)),
                  pl.BlockSpec((1, d), lambda i: (0, 0))],
        out_specs=pl.BlockSpec((tm, d), lambda i: (i, 0)),
        out_shape=jax.ShapeDtypeStruct((s, d), BF16),
        compiler_params=_params("parallel"),
        name="rmsnorm",
    )(x, g.reshape(1, d))


def _add_rmsnorm_kernel(x_ref, y_ref, g1_ref, g2_ref, xo_ref, ho_ref):
    xn = x_ref[...] + _rms(y_ref[...], g1_ref[...])
    xo_ref[...] = xn
    ho_ref[...] = _rms(xn, g2_ref[...]).astype(ho_ref.dtype)


def add_rmsnorm(x, y, g_post, g_pre, tm=256):
    s, d = x.shape
    row = pl.BlockSpec((tm, d), lambda i: (i, 0))
    vec = pl.BlockSpec((1, d), lambda i: (0, 0))
    return pl.pallas_call(
        _add_rmsnorm_kernel,
        grid=(s // tm,),
        in_specs=[row, row, vec, vec],
        out_specs=[row, row],
        out_shape=[jax.ShapeDtypeStruct((s, d), F32), jax.ShapeDtypeStruct((s, d), BF16)],
        compiler_params=_params("parallel"),
        name="add_rmsnorm",
    )(x, y, g_post.reshape(1, d), g_pre.reshape(1, d))


def _mm_kernel(a_ref, b_ref, o_ref):
    o_ref[...] = jnp.dot(a_ref[...], b_ref[...], preferred_element_type=F32).astype(o_ref.dtype)


def matmul(a, b, out_dtype, tm=512, tn=512):
    m, k = a.shape
    _, n = b.shape
    return pl.pallas_call(
        _mm_kernel,
        grid=(m // tm, n // tn),
        in_specs=[pl.BlockSpec((tm, k), lambda i, j: (i, 0)),
                  pl.BlockSpec((k, tn), lambda i, j: (0, j))],
        out_specs=pl.BlockSpec((tm, tn), lambda i, j: (i, j)),
        out_shape=jax.ShapeDtypeStruct((m, n), out_dtype),
        compiler_params=_params("parallel", "parallel"),
        name="matmul",
    )(a, b)


def _mm2_kernel(a1_ref, a2_ref, b1_ref, b2_ref, o_ref):
    acc = jnp.dot(a1_ref[...], b1_ref[...], preferred_element_type=F32)
    acc += jnp.dot(a2_ref[...], b2_ref[...], preferred_element_type=F32)
    o_ref[...] = acc.astype(o_ref.dtype)


def matmul_cat(a1, a2, b, out_dtype, tm=512, tn=512):
    m, k1 = a1.shape
    _, k2 = a2.shape
    assert k1 == k2 and b.shape[0] == k1 + k2
    n = b.shape[1]
    return pl.pallas_call(
        _mm2_kernel,
        grid=(m // tm, n // tn),
        in_specs=[pl.BlockSpec((tm, k1), lambda i, j: (i, 0)),
                  pl.BlockSpec((tm, k2), lambda i, j: (i, 0)),
                  pl.BlockSpec((k1, tn), lambda i, j: (0, j)),
                  pl.BlockSpec((k2, tn), lambda i, j: (1, j))],
        out_specs=pl.BlockSpec((tm, tn), lambda i, j: (i, j)),
        out_shape=jax.ShapeDtypeStruct((m, n), out_dtype),
        compiler_params=_params("parallel", "parallel"),
        name="matmul_cat",
    )(a1, a2, b, b)


def _shift_rows(x, prev, d):
    y = pltpu.roll(x, d, 0)
    row = lax.broadcasted_iota(jnp.int32, x.shape, 0)
    for r in range(d):
        y = jnp.where(row == r, prev[SUBLANES - d + r:SUBLANES - d + r + 1, :], y)
    return y


def _ffn_up_kernel(h_ref, wg_ref, wv_ref, cg_ref, cv_ref, o_ref, hg_ref, hv_ref):
    i = pl.program_id(1)

    @pl.when(i == 0)
    def _():
        hg_ref[...] = jnp.zeros_like(hg_ref)
        hv_ref[...] = jnp.zeros_like(hv_ref)

    h = h_ref[...]

    def conv(w_ref, c_ref, halo_ref):
        raw = jnp.dot(h, w_ref[...], preferred_element_type=F32)
        prev = halo_ref[...]
        c = c_ref[...]
        out = (c[2:3, :] * raw + c[1:2, :] * _shift_rows(raw, prev, 1)
               + c[0:1, :] * _shift_rows(raw, prev, 2))
        halo_ref[...] = raw[raw.shape[0] - SUBLANES:, :]
        return out

    gate = conv(wg_ref, cg_ref, hg_ref)
    val = conv(wv_ref, cv_ref, hv_ref)
    o_ref[...] = (jax.nn.gelu(gate, approximate=True) * val).astype(o_ref.dtype)


def ffn_up(h, w_up, conv_w, tm=512, tn=512):
    s, d = h.shape
    f = w_up.shape[1] // 2
    nj = f // tn
    return pl.pallas_call(
        _ffn_up_kernel,
        grid=(nj, s // tm),
        in_specs=[pl.BlockSpec((tm, d), lambda j, i: (i, 0)),
                  pl.BlockSpec((d, tn), lambda j, i: (0, j)),
                  pl.BlockSpec((d, tn), lambda j, i: (0, j + nj)),
                  pl.BlockSpec((FFN_CONV, tn), lambda j, i: (0, j)),
                  pl.BlockSpec((FFN_CONV, tn), lambda j, i: (0, j + nj))],
        out_specs=pl.BlockSpec((tm, tn), lambda j, i: (i, j)),
        out_shape=jax.ShapeDtypeStruct((s, f), BF16),
        scratch_shapes=[pltpu.VMEM((SUBLANES, tn), F32), pltpu.VMEM((SUBLANES, tn), F32)],
        compiler_params=_params("parallel", "arbitrary"),
        name="ffn_up",
    )(h, w_up, w_up, conv_w, conv_w)


def _dot_nt(a, b):
    return lax.dot_general(a, b, (((1,), (1,)), ((), ())), preferred_element_type=F32)


def _moba_kernel(slope_ref, q_ref, k_ref, v_ref, o_ref, kbar_ref, vt_ref,
                 s0_buf, s1_buf, p0_buf, p1_buf, *, n_blk):
    hd = pl.program_id(0)
    g = pl.program_id(1)
    bs = MOBA_BLOCK
    slope = slope_ref[hd]

    def pair_rows(ref, t):
        return ref[pl.ds(pl.multiple_of(t * (2 * bs), 2 * bs), 2 * bs), :]

    @pl.when(g == 0)
    def _():
        def kmean(j, c):
            kb = k_ref[pl.ds(pl.multiple_of(j * bs, bs), bs), :].astype(F32)
            kbar_ref[pl.ds(j, 1), :] = jnp.mean(kb, axis=0, keepdims=True)
            return c
        lax.fori_loop(0, n_blk, kmean, 0)

        def vtrans(t, c):
            vt_ref[t] = pair_rows(v_ref, t).astype(F32).T.astype(BF16)
            return c
        lax.fori_loop(0, n_blk // 2, vtrans, 0)

    q = q_ref[...]
    second = lax.broadcasted_iota(jnp.int32, (1, 2 * bs), 1) >= bs
    own = 2 * g + second.astype(jnp.int32)
    kbar = kbar_ref[...]
    kb_hi = kbar.astype(BF16)
    kb_lo = (kbar - kb_hi.astype(F32)).astype(BF16)
    gate = _dot_nt(kb_hi, q) + _dot_nt(kb_lo, q)
    blk = lax.broadcasted_iota(jnp.int32, gate.shape, 0)
    gate = jnp.where(blk < own, gate, -jnp.inf)
    sel = []
    for _ in range(MOBA_TOPK):
        m = jnp.max(gate, axis=0, keepdims=True)
        cand = (gate == m) & (gate > -jnp.inf)
        idx = jnp.min(jnp.where(cand, blk, n_blk), axis=0, keepdims=True)
        sel.append(idx)
        gate = jnp.where(blk == idx, -jnp.inf, gate)

    slope2 = slope * LOG2E
    key = lax.broadcasted_iota(jnp.int32, (2 * bs, 2 * bs), 0)
    qry = lax.broadcasted_iota(jnp.int32, (2 * bs, 2 * bs), 1)
    cb_pair = slope2 * key.astype(F32)

    def qbias(j):
        picked = (sel[0] == j) | (sel[1] == j) | (sel[2] == j)
        return jnp.where(picked, 0.0, NEG_BIG)

    s = _dot_nt(pair_rows(k_ref, g), q) + cb_pair
    ok_first = jnp.where(second, qbias(2 * g), 0.0)
    s = jnp.where(key <= qry, s, NEG_BIG)
    s_a, s_b = s[:bs] + ok_first, s[bs:]
    m0 = jnp.maximum(jnp.max(s_a, axis=0, keepdims=True), jnp.max(s_b, axis=0, keepdims=True))
    p = jnp.exp2(jnp.concatenate([s_a, s_b], axis=0) - m0)
    l0 = jnp.sum(p, axis=0, keepdims=True)
    acc0 = jnp.dot(vt_ref[g], p.astype(BF16), preferred_element_type=F32)

    n_it = g
    t_last = jnp.maximum(n_it - 1, 0)

    s_buf = (s0_buf, s1_buf)
    p_buf = (p0_buf, p1_buf)
    s_buf[0][...] = _dot_nt(pair_rows(k_ref, 0), q)
    p_buf[1][...] = jnp.zeros((2 * bs, 2 * bs), BF16)

    def phase(t, carry, cur):
        m_i, l_i, acc = carry
        s_buf[1 - cur][...] = _dot_nt(pair_rows(k_ref, jnp.minimum(t + 1, t_last)), q)
        pv = jnp.dot(vt_ref[jnp.clip(t - 1, 0, t_last)], p_buf[1 - cur][...],
                     preferred_element_type=F32)
        ja = 2 * t
        s = s_buf[cur][...] + cb_pair
        sa, sb = s[:bs], s[bs:]
        off = jnp.where(t < n_it, slope2 * ((ja - 2 * g) * bs).astype(F32), NEG_BIG)
        qa, qb = qbias(ja) + off, qbias(ja + 1) + off
        m_new = jnp.maximum(m_i, jnp.maximum(jnp.max(sa, axis=0, keepdims=True) + qa,
                                             jnp.max(sb, axis=0, keepdims=True) + qb))
        pa = jnp.exp2(sa + (qa - m_new))
        pb = jnp.exp2(sb + (qb - m_new))
        alpha = jnp.exp2(m_i - m_new)
        l_new = (alpha * l_i + jnp.sum(pa, axis=0, keepdims=True)
                 + jnp.sum(pb, axis=0, keepdims=True))
        p_buf[cur][...] = jnp.concatenate([pa, pb], axis=0).astype(BF16)
        return m_new, l_new, alpha * (acc + pv)

    def body(u, carry):
        for ph in range(MOBA_PHASES):
            carry = phase(MOBA_PHASES * u + ph, carry, ph % 2)
        return carry

    n_trip = (n_it + MOBA_PHASES - 1) // MOBA_PHASES
    _, l_f, acc_f = lax.fori_loop(0, n_trip, body, (m0, l0, acc0))
    acc_f = acc_f + jnp.dot(vt_ref[jnp.clip(MOBA_PHASES * n_trip - 1, 0, t_last)], p_buf[1][...],
                            preferred_element_type=F32)
    o_ref[...] = (acc_f / l_f).T.astype(o_ref.dtype)


def moba_attention(proj, n_heads):
    s = proj.shape[0]
    dh, bs = MOBA_HEAD_DIM, MOBA_BLOCK
    n_blk = s // bs
    slopes = 2.0 ** (-8.0 * jnp.arange(1, n_heads + 1, dtype=F32) / n_heads)
    grid_spec = pltpu.PrefetchScalarGridSpec(
        num_scalar_prefetch=1,
        grid=(n_heads, n_blk // 2),
        in_specs=[pl.BlockSpec((2 * bs, dh), lambda h, g, sl: (g, h)),
                  pl.BlockSpec((s, dh), lambda h, g, sl: (0, n_heads + h)),
                  pl.BlockSpec((s, dh), lambda h, g, sl: (0, 2 * n_heads + h))],
        out_specs=pl.BlockSpec((2 * bs, dh), lambda h, g, sl: (g, h)),
        scratch_shapes=[pltpu.VMEM((n_blk, dh), F32),
                        pltpu.VMEM((n_blk // 2, dh, 2 * bs), BF16),
                        pltpu.VMEM((2 * bs, 2 * bs), F32), pltpu.VMEM((2 * bs, 2 * bs), F32),
                        pltpu.VMEM((2 * bs, 2 * bs), BF16), pltpu.VMEM((2 * bs, 2 * bs), BF16)],
    )
    return pl.pallas_call(
        functools.partial(_moba_kernel, n_blk=n_blk),
        grid_spec=grid_spec,
        out_shape=jax.ShapeDtypeStruct((s, n_heads * dh), BF16),
        compiler_params=_params("parallel", "arbitrary"),
        name="moba",
    )(slopes, proj, proj, proj)


def _softplus(z):
    return jnp.maximum(z, 0.0) + jnp.log1p(jnp.exp(-jnp.abs(z)))


def _sb_kernel(q_ref, k_ref, v_ref, o_ref, z0_buf, z1_buf, w0_buf, w1_buf, *, tb):
    i = pl.program_id(1)
    q = q_ref[...]
    row = lax.broadcasted_iota(jnp.int32, (tb, tb), 0)
    col = lax.broadcasted_iota(jnp.int32, (tb, tb), 1)
    tri = jnp.where(row >= col, 1.0, 0.0).astype(BF16)
    past = col < row

    def weights(z2, later, masked):
        nz = -z2
        lk = jnp.minimum(nz, 0.0) - jnp.log2(1.0 + jnp.exp2(jnp.minimum(z2, nz)))
        if masked:
            lk = jnp.where(past, lk, 0.0)
        c = jnp.dot(lk.astype(BF16), tri, preferred_element_type=F32)
        w = jnp.exp2(z2 + c + later)
        if masked:
            w = jnp.where(past, w, 0.0)
        return w.astype(BF16), jnp.sum(lk, axis=-1, keepdims=True)

    i0 = pl.multiple_of(i * tb, tb)
    w0, later0 = weights(_dot_nt(q, k_ref[pl.ds(i0, tb), :]), jnp.zeros((tb, 1), F32), True)
    acc0 = jnp.dot(w0, v_ref[pl.ds(i0, tb), :], preferred_element_type=F32)

    n_pair = i // 2
    n_trip = (n_pair + SB_PAIRS - 1) // SB_PAIRS
    t_last = jnp.maximum(n_pair - 1, 0)
    z_buf = (z0_buf, z1_buf)
    w_buf = (w0_buf, w1_buf)

    def pair_rows(ref, t):
        j0 = jnp.maximum(i - 2 - 2 * t, 0) * tb
        return ref[pl.ds(pl.multiple_of(j0, tb), 2 * tb), :]

    z_buf[0][...] = _dot_nt(q, pair_rows(k_ref, 0))
    w_buf[1][...] = jnp.zeros((tb, 2 * tb), BF16)

    def pair(t, carry, cur):
        acc, later = carry
        z_buf[1 - cur][...] = _dot_nt(q, pair_rows(k_ref, jnp.minimum(t + 1, t_last)))
        acc = acc + jnp.dot(w_buf[1 - cur][...], pair_rows(v_ref, jnp.clip(t - 1, 0, t_last)),
                            preferred_element_type=F32)
        z2 = z_buf[cur][...]
        valid = t < n_pair
        later_hi = jnp.where(valid, later, NEG_BIG)
        w_hi, tot_hi = weights(z2[:, tb:], later_hi, False)
        w_lo, tot_lo = weights(z2[:, :tb], later_hi + tot_hi, False)
        w_buf[cur][...] = jnp.concatenate([w_lo, w_hi], axis=1)
        return acc, later + jnp.where(valid, tot_hi + tot_lo, 0.0)

    def trip(u, carry):
        for ph in range(SB_PAIRS):
            carry = pair(SB_PAIRS * u + ph, carry, ph % 2)
        return carry

    acc, later = lax.fori_loop(0, n_trip, trip, (acc0, later0))
    acc = acc + jnp.dot(w_buf[1][...], pair_rows(v_ref, jnp.clip(SB_PAIRS * n_trip - 1, 0, t_last)),
                        preferred_element_type=F32)

    def single(t, carry):
        acc, later = carry
        w, tot = weights(_dot_nt(q, k_ref[pl.ds(0, tb), :]), later, False)
        return acc + jnp.dot(w, v_ref[pl.ds(0, tb), :], preferred_element_type=F32), later + tot

    acc, _ = lax.fori_loop(0, i - 2 * n_pair, single, (acc, later))
    o_ref[...] = acc.astype(o_ref.dtype)


def sb_attention(proj, n_heads, col0, tb=256):
    s = proj.shape[0]
    dh = SB_HEAD_DIM
    c0 = col0 // dh
    return pl.pallas_call(
        functools.partial(_sb_kernel, tb=tb),
        grid=(n_heads, s // tb),
        in_specs=[pl.BlockSpec((tb, dh), lambda h, i: (i, c0 + h)),
                  pl.BlockSpec((s, dh), lambda h, i: (0, c0 + n_heads + h)),
                  pl.BlockSpec((s, dh), lambda h, i: (0, c0 + 2 * n_heads + h))],
        out_specs=pl.BlockSpec((tb, dh), lambda h, i: (i, h)),
        out_shape=jax.ShapeDtypeStruct((s, n_heads * dh), BF16),
        scratch_shapes=[pltpu.VMEM((tb, 2 * tb), F32), pltpu.VMEM((tb, 2 * tb), F32),
                        pltpu.VMEM((tb, 2 * tb), BF16), pltpu.VMEM((tb, 2 * tb), BF16)],
        compiler_params=_params("parallel", "arbitrary"),
        name="stickbreak",
    )(proj, proj, proj)


RWKV_W = 2048
LORA_W = 640
HEAD_GROUP = 256
RWKV_CHUNK = 64


def _sigmoid(x):
    return 1.0 / (1.0 + jnp.exp(-x))


def _head_ones(n):
    r = lax.broadcasted_iota(jnp.int32, (n, n), 0) // RWKV_HEAD
    c = lax.broadcasted_iota(jnp.int32, (n, n), 1) // RWKV_HEAD
    return jnp.where(r == c, 1.0, 0.0).astype(BF16)


def _head_sum(x):
    e = _head_ones(HEAD_GROUP)
    hi = x.astype(BF16)
    lo = (x - hi.astype(F32)).astype(BF16)
    outs = []
    for c in range(x.shape[1] // HEAD_GROUP):
        sl = slice(c * HEAD_GROUP, (c + 1) * HEAD_GROUP)
        outs.append(jnp.dot(hi[:, sl], e, preferred_element_type=F32)
                    + jnp.dot(lo[:, sl], e, preferred_element_type=F32))
    return jnp.concatenate(outs, axis=-1)


def _token_shift(x_ref, halo_ref, mu_ref, first):
    x = x_ref[...]
    prev = jnp.where(first, 0.0, halo_ref[...])
    return x + (_shift_rows(x, prev, 1) - x) * mu_ref[...]


def _rwkv_prep_kernel(*refs, vres):
    (r_ref, k_ref, v_ref, lo_ref, rh_ref, kh_ref, vh_ref, loh_ref,
     mur_ref, muk_ref, muv_ref, mulo_ref,
     w0_ref, w2_ref, a0_ref, a2_ref, v0_ref, v2_ref, g2_ref, kk_ref, ka_ref, rk_ref) = refs[:22]
    rest = refs[22:]
    if vres:
        vf_ref, rest = rest[0], rest[1:]
    ro_ref, lw_ref, ko_ref, vo_ref, na_ref, bb_ref, g_ref, bo_ref = rest
    first = pl.program_id(0) == 0
    r = _token_shift(r_ref, rh_ref, mur_ref, first)
    k = _token_shift(k_ref, kh_ref, muk_ref, first)
    v = _token_shift(v_ref, vh_ref, muv_ref, first)
    lo = _token_shift(lo_ref, loh_ref, mulo_ref, first)
    w_lo, a_lo, v_lo, g_lo = lo[:, 0:128], lo[:, 128:256], lo[:, 256:384], lo[:, 384:640]

    def lora(x, w_ref):
        return jnp.dot(x.astype(BF16), w_ref[...], preferred_element_type=F32)

    w_log = -_softplus(-(w0_ref[...] + lora(jnp.tanh(w_lo), w2_ref))) - 0.5
    lw_ref[...] = -jnp.exp(w_log)
    if vres:
        v = v + (vf_ref[...] - v) * _sigmoid(v0_ref[...] + lora(v_lo, v2_ref))
    a = _sigmoid(a0_ref[...] + lora(a_lo, a2_ref))
    g_ref[...] = lora(_sigmoid(g_lo), g2_ref)
    kk = k * kk_ref[...]
    kk = kk / jnp.maximum(jnp.sqrt(_head_sum(kk * kk)), 1e-12)
    k2 = k * (1.0 + (a - 1.0) * ka_ref[...])
    ro_ref[...] = r
    ko_ref[...] = k2
    vo_ref[...] = v
    na_ref[...] = -kk
    bb_ref[...] = kk * a
    bo_ref[...] = _head_sum(r * k2 * rk_ref[...]) * v


def rwkv_prep(proj, col0, mu, w0, w2p, a0, a2p, v0, v2p, g2, k_k, k_a, r_k, v_first, tm=128):
    s = proj.shape[0]
    w = RWKV_W
    vres = v_first is not None
    c_r, c_lo = col0 // w, (col0 + 3 * w) // LORA_W
    hb = tm // SUBLANES

    def main(width, cidx):
        return pl.BlockSpec((tm, width), lambda i: (i, cidx))

    def halo(width, cidx):
        return pl.BlockSpec((SUBLANES, width), lambda i: (jnp.maximum(i * hb - 1, 0), cidx))

    def vec(width, cidx=0):
        return pl.BlockSpec((1, width), lambda i: (0, cidx))

    def full(shape):
        return pl.BlockSpec(shape, lambda i: (0, 0))

    mu2 = mu.reshape(1, -1)
    r1 = lambda t: t.reshape(1, -1)
    in_specs = [main(w, c_r), main(w, c_r + 1), main(w, c_r + 2), main(LORA_W, c_lo),
                halo(w, c_r), halo(w, c_r + 1), halo(w, c_r + 2), halo(LORA_W, c_lo),
                vec(w, c_r), vec(w, c_r + 1), vec(w, c_r + 2), vec(LORA_W, c_lo),
                vec(w), full((128, w)), vec(w), full((128, w)), vec(w), full((128, w)),
                full((256, w)), vec(w), vec(w), vec(w)]
    args = [proj] * 8 + [mu2] * 4 + [r1(w0), w2p, r1(a0), a2p, r1(v0), v2p, g2, r1(k_k), r1(k_a), r1(r_k)]
    if vres:
        in_specs.append(main(w, 0))
        args.append(v_first)
    out = pl.BlockSpec((tm, w), lambda i: (i, 0))
    return pl.pallas_call(
        functools.partial(_rwkv_prep_kernel, vres=vres),
        grid=(s // tm,),
        in_specs=in_specs,
        out_specs=[out] * 8,
        out_shape=[jax.ShapeDtypeStruct((s, w), F32)] * 8,
        compiler_params=_params("parallel"),
        name="rwkv_prep",
    )(*args)


def _cumsum_rows(x):
    n = x.shape[0]
    row = lax.broadcasted_iota(jnp.int32, x.shape, 0)
    d = 1
    while d < n:
        x = x + jnp.where(row >= d, pltpu.roll(x, d, 0), 0.0)
        d *= 2
    return x


def _rwkv_chunk(r, lw, k, v, a, b, state):
    L, n2 = r.shape
    nh = n2 // RWKV_HEAD
    J = nh * L
    bd = lambda x: x.astype(BF16)
    mm = lambda x, y: jnp.dot(bd(x), bd(y), preferred_element_type=F32)
    lg = _cumsum_rows(lw)
    tot = lg[L - 1:L, :]
    e_neg = jnp.exp(-lg)
    e_hat = jnp.exp(tot - lg)
    a_t = a * jnp.exp(lg - lw)
    r_t = r * jnp.exp(lg)
    b_i, k_i = b * e_neg, k * e_neg
    b_hat, k_hat = b * e_hat, k * e_hat

    head = lax.broadcasted_iota(jnp.int32, (L, n2), 1) // RWKV_HEAD

    def stack(x):
        return jnp.concatenate([jnp.where(head == h, x, 0.0) for h in range(nh)], axis=0)

    def fold(x):
        out = x[:L]
        for h in range(1, nh):
            out = out + x[h * L:(h + 1) * L]
        return out

    row = lax.broadcasted_iota(jnp.int32, (J, J), 0)
    col = lax.broadcasted_iota(jnp.int32, (J, J), 1)
    same = (row // L) == (col // L)
    strict = same & (col < row)
    incl = same & (col <= row)
    eye = jnp.where(row == col, 1.0, 0.0)

    x_a, x_r = stack(a_t), stack(r_t)
    ybk = jnp.concatenate([b_i] * nh + [k_i] * nh, axis=0)
    pa = _dot_nt(bd(x_a), bd(ybk))
    pr = _dot_nt(bd(x_r), bd(ybk))
    yield
    m_ab = jnp.where(strict, pa[:, :J], 0.0)
    m_ak = jnp.where(strict, pa[:, J:], 0.0)
    m_rb = jnp.where(incl, pr[:, :J], 0.0)
    m_rk = jnp.where(incl, pr[:, J:], 0.0)

    v_st = stack(v)
    w1 = mm(m_ak, v_st)
    t_inv, pw = eye + m_ab, m_ab
    for _ in range(int(math.log2(L)) - 1):
        pw = mm(pw, pw)
        yield
        t_inv = mm(t_inv, eye + pw)

    yield
    tw = mm(t_inv, jnp.concatenate([x_a, w1], axis=1))
    yield
    ta, w0 = fold(tw[:, :n2]), fold(tw[:, n2:])
    gs = _dot_nt(bd(jnp.concatenate([ta, r_t], axis=0)), bd(state))
    lhs_t = jnp.concatenate([jnp.concatenate([ta, jnp.zeros_like(ta)], axis=0).T,
                             jnp.concatenate([w0, v], axis=0).T], axis=0)
    pq = mm(lhs_t, jnp.concatenate([b_hat, k_hat], axis=0))
    yield
    u = gs[:L] + w0
    yj = mm(jnp.concatenate([m_rb, m_rk], axis=1), jnp.concatenate([stack(u), v_st], axis=0))
    y = gs[L:] + fold(yj)
    yield
    ch_r = lax.broadcasted_iota(jnp.int32, (n2, n2), 0)
    ch_c = lax.broadcasted_iota(jnp.int32, (n2, n2), 1)
    head_blk = (ch_r // RWKV_HEAD) == (ch_c // RWKV_HEAD)
    p_mat = jnp.where(head_blk, pq[:n2], 0.0) + jnp.where(ch_r == ch_c, jnp.exp(tot), 0.0)
    q_mat = jnp.where(head_blk, pq[n2:], 0.0)
    return y, mm(state, p_mat) + q_mat


def _run_lockstep(chains):
    results = [None] * len(chains)
    live = list(range(len(chains)))
    while live:
        for g in list(live):
            try:
                next(chains[g])
            except StopIteration as done:
                results[g] = done.value
                live.remove(g)
    return results


def _rwkv_rec_kernel(r_ref, lw_ref, k_ref, v_ref, a_ref, b_ref, y_ref, st_ref, *, n_groups, n_chunks):
    gw = st_ref.shape[-1]

    @pl.when(pl.program_id(1) == 0)
    def _():
        st_ref[...] = jnp.zeros_like(st_ref)

    def body(c, carry):
        rows = pl.ds(pl.multiple_of(c * RWKV_CHUNK, RWKV_CHUNK), RWKV_CHUNK)
        in_refs = (r_ref, lw_ref, k_ref, v_ref, a_ref, b_ref)
        chains = [_rwkv_chunk(*[ref[rows, p * gw:(p + 1) * gw] for ref in in_refs], st_ref[p])
                  for p in range(n_groups)]
        for p, (y, st) in enumerate(_run_lockstep(chains)):
            y_ref[rows, p * gw:(p + 1) * gw] = y
            st_ref[p] = st
        return carry

    lax.fori_loop(0, n_chunks, body, 0)


def rwkv_recurrence(r, lw, k, v, a, b, tt=512, n_groups=4, heads_per_group=4):
    s, w = r.shape
    gw = heads_per_group * RWKV_HEAD
    wb = n_groups * gw
    blk = pl.BlockSpec((tt, wb), lambda h, t: (t, h))
    return pl.pallas_call(
        functools.partial(_rwkv_rec_kernel, n_groups=n_groups, n_chunks=tt // RWKV_CHUNK),
        grid=(w // wb, s // tt),
        in_specs=[blk] * 6,
        out_specs=blk,
        out_shape=jax.ShapeDtypeStruct((s, w), F32),
        scratch_shapes=[pltpu.VMEM((n_groups, gw, gw), F32)],
        compiler_params=_params("arbitrary", "arbitrary"),
        name="rwkv_rec",
    )(r, lw, k, v, a, b)


def _rwkv_post_kernel(y_ref, g_ref, bo_ref, lw_ref, lb_ref, o_ref):
    y = y_ref[...]
    inv_n = 1.0 / RWKV_HEAD
    mu = _head_sum(y) * inv_n
    yc = y - mu
    var = _head_sum(yc * yc) * inv_n
    yn = yc * lax.rsqrt(var + GN_EPS) * lw_ref[...] + lb_ref[...]
    o_ref[...] = ((yn + bo_ref[...]) * g_ref[...]).astype(o_ref.dtype)


def rwkv_post(y, g, bonus, lnx_w, lnx_b, tm=256):
    s, w = y.shape
    row = pl.BlockSpec((tm, w), lambda i: (i, 0))
    vec = pl.BlockSpec((1, w), lambda i: (0, 0))
    return pl.pallas_call(
        _rwkv_post_kernel,
        grid=(s // tm,),
        in_specs=[row, row, row, vec, vec],
        out_specs=row,
        out_shape=jax.ShapeDtypeStruct((s, w), BF16),
        compiler_params=_params("parallel"),
        name="rwkv_post",
    )(y, g, bonus, lnx_w.reshape(1, w), lnx_b.reshape(1, w))


def _expm1(y):
    u = jnp.exp(y)
    near = (u - 1.0) * y / jnp.log(u)
    return jnp.where(u == 1.0, y, jnp.where(y < -0.5, u - 1.0, near))


def _lru_kernel(gate_ref, x_ref, xh_ref, cw_ref, cb_ref, gw_ref, gb_ref, lam_ref, o_ref, h_ref, *, nb):
    t = pl.program_id(1)

    @pl.when(t == 0)
    def _():
        h_ref[...] = jnp.zeros_like(h_ref)

    x = x_ref[...]
    prev = jnp.where(t == 0, 0.0, xh_ref[...])
    cw = cw_ref[...]
    xb = cw[3:4, :] * x + cb_ref[...]
    for d in range(1, LRU_CONV):
        xb = xb + cw[3 - d:4 - d, :] * _shift_rows(x, prev, d)
    xb16 = xb.astype(BF16)
    g0, g1 = [], []
    for n in range(nb):
        xs = xb16[:, n * LANES:(n + 1) * LANES]
        g0.append(jnp.dot(xs, gw_ref[0, n], preferred_element_type=F32))
        g1.append(jnp.dot(xs, gw_ref[1, n], preferred_element_type=F32))
    gb = gb_ref[...]
    g0 = jnp.concatenate(g0, axis=-1) + gb[0:1, :]
    g1 = jnp.concatenate(g1, axis=-1) + gb[1:2, :]
    log_a = -LRU_C * _sigmoid(g0) * _softplus(-lam_ref[...])
    a = jnp.exp(log_a)
    b = jnp.sqrt(-_expm1(2.0 * log_a)) * _sigmoid(g1) * xb
    n = x.shape[0]
    row = lax.broadcasted_iota(jnp.int32, x.shape, 0)
    d = 1
    while d < n:
        keep = row >= d
        a_s = jnp.where(keep, pltpu.roll(a, d, 0), 1.0)
        b_s = jnp.where(keep, pltpu.roll(b, d, 0), 0.0)
        b = a * b_s + b
        a = a * a_s
        d *= 2
    h = b + a * h_ref[SUBLANES - 1:SUBLANES, :]
    h_ref[...] = h[n - SUBLANES:, :]
    o_ref[...] = (h * jax.nn.gelu(gate_ref[...], approximate=True)).astype(o_ref.dtype)


def rg_lru(proj, conv_w, conv_b, gate_w, gate_b, lam, tm=256, tc=512):
    s = proj.shape[0]
    w = conv_w.shape[1]
    nj = w // tc
    nb = tc // LANES
    hb = tm // SUBLANES
    return pl.pallas_call(
        functools.partial(_lru_kernel, nb=nb),
        grid=(nj, s // tm),
        in_specs=[pl.BlockSpec((tm, tc), lambda j, t: (t, j)),
                  pl.BlockSpec((tm, tc), lambda j, t: (t, nj + j)),
                  pl.BlockSpec((SUBLANES, tc), lambda j, t: (jnp.maximum(t * hb - 1, 0), nj + j)),
                  pl.BlockSpec((LRU_CONV, tc), lambda j, t: (0, j)),
                  pl.BlockSpec((1, tc), lambda j, t: (0, j)),
                  pl.BlockSpec((2, nb, LANES, LANES), lambda j, t: (0, j, 0, 0)),
                  pl.BlockSpec((2, tc), lambda j, t: (0, j)),
                  pl.BlockSpec((1, tc), lambda j, t: (0, j))],
        out_specs=pl.BlockSpec((tm, tc), lambda j, t: (t, j)),
        out_shape=jax.ShapeDtypeStruct((s, w), BF16),
        scratch_shapes=[pltpu.VMEM((SUBLANES, tc), F32)],
        compiler_params=_params("arbitrary", "arbitrary"),
        name="rg_lru",
    )(proj, proj, proj, conv_w, conv_b.reshape(1, w), gate_w.astype(BF16), gate_b, lam.reshape(1, w))


def even_mixer(h, w_in, w_out):
    half = w_out.shape[0] // 2
    n_moba = half // MOBA_HEAD_DIM
    n_sb = half // SB_HEAD_DIM
    col = jnp.arange(w_in.shape[1])
    qscale = jnp.where(col < half, MOBA_HEAD_DIM ** -0.5 * LOG2E,
                       jnp.where((col >= 3 * half) & (col < 4 * half), SB_HEAD_DIM ** -0.5 * LOG2E, 1.0))
    proj = matmul(h, (w_in * qscale.astype(F32)).astype(BF16), BF16)
    oa = moba_attention(proj, n_moba)
    ob = sb_attention(proj, n_sb, 3 * n_moba * MOBA_HEAD_DIM)
    return matmul_cat(oa, ob, w_out.astype(BF16), F32)


def _pad_cols(t, width):
    return jnp.pad(t, [(0, 0)] * (t.ndim - 1) + [(0, width - t.shape[-1])])


def _odd_layout(t, n_c, vres):
    w = RWKV_W
    sizes = [96, 96] + ([64] if vres else []) + [256]
    segs, c = [], 3 * w
    for sz in sizes:
        segs.append(_pad_cols(t[..., c:c + sz], LANES if sz < LANES else sz))
        c += sz
    if not vres:
        segs.insert(2, jnp.zeros(t.shape[:-1] + (LANES,), t.dtype))
    assert c == n_c
    return jnp.concatenate([t[..., n_c:], t[..., :3 * w]] + segs, axis=-1)


def _pad_rows(t, rows):
    return jnp.pad(t, ((0, rows - t.shape[0]), (0, 0)))


def odd_mixer(h, v_first, p):
    w = RWKV_W
    vres = "v2" in p
    n_c = p["shift_mu"].shape[0]
    w_in = _odd_layout(p["w_in"], n_c, vres).astype(BF16)
    lru_w = p["w_in"].shape[1] - n_c
    mu = _odd_layout(jnp.concatenate([p["shift_mu"], jnp.zeros((lru_w,), F32)]), n_c, vres)
    proj = matmul(h, w_in, F32, tn=LORA_W)
    zeros_w = jnp.zeros((w,), F32)
    r, lw, k, v, na, bb, g, bonus = rwkv_prep(
        proj, lru_w, mu, p["w0"], _pad_rows(p["w2"], LANES).astype(BF16),
        p["a0"], _pad_rows(p["a2"], LANES).astype(BF16),
        p["v0"] if vres else zeros_w,
        _pad_rows(p["v2"], LANES).astype(BF16) if vres else jnp.zeros((LANES, w), BF16),
        p["g2"].astype(BF16), p["k_k"], p["k_a"], p["r_k"].reshape(-1), v_first)
    if v_first is None:
        v_first = v
    y = rwkv_recurrence(r, lw, k, v, na, bb)
    c_out = rwkv_post(y, g, bonus, p["lnx_w"], p["lnx_b"])
    d_out = rg_lru(proj, p["conv_w"], p["conv_b"], p["gate_w"], p["gate_b"], p["lru_lambda"])
    return matmul_cat(c_out, d_out, p["w_out"].astype(BF16), F32), v_first


_ODD_NAMES = ("w_in", "w_out", "shift_mu", "w0", "w2", "a0", "a2", "g2", "k_k", "k_a", "r_k",
              "lnx_w", "lnx_b", "conv_w", "conv_b", "gate_w", "gate_b", "lru_lambda")


def _forward(x, layers):
    x2 = x.reshape(x.shape[-2], x.shape[-1])
    h = rmsnorm(x2, layers[0]["norms"][0])
    v_first = None
    for li, p in enumerate(layers):
        g = p["norms"]
        if "shift_mu" in p:
            mix, v_first = odd_mixer(h, v_first, p)
        else:
            mix = even_mixer(h, p["w_in"], p["w_out"])
        x2, h = add_rmsnorm(x2, mix, g[1], g[2])
        act = ffn_up(h, p["ffn_up"].astype(BF16), p["ffn_conv"])
        y = matmul(act, p["ffn_down"].astype(BF16), F32)
        g_next = layers[li + 1]["norms"][0] if li + 1 < len(layers) else g[3]
        x2, h = add_rmsnorm(x2, y, g[3], g_next)
    return x2.reshape(x.shape)


def kernel(x, l0_norms, l0_w_in, l0_w_out, l0_ffn_up, l0_ffn_conv, l0_ffn_down, l1_norms, l1_w_in, l1_w_out, l1_shift_mu, l1_w0, l1_w2, l1_a0, l1_a2, l1_g2, l1_k_k, l1_k_a, l1_r_k, l1_lnx_w, l1_lnx_b, l1_conv_w, l1_conv_b, l1_gate_w, l1_gate_b, l1_lru_lambda, l1_ffn_up, l1_ffn_conv, l1_ffn_down, l2_norms, l2_w_in, l2_w_out, l2_ffn_up, l2_ffn_conv, l2_ffn_down, l3_norms, l3_w_in, l3_w_out, l3_shift_mu, l3_w0, l3_w2, l3_a0, l3_a2, l3_v0, l3_v2, l3_g2, l3_k_k, l3_k_a, l3_r_k, l3_lnx_w, l3_lnx_b, l3_conv_w, l3_conv_b, l3_gate_w, l3_gate_b, l3_lru_lambda, l3_ffn_up, l3_ffn_conv, l3_ffn_down):
    a = locals()
    layers = []
    for li in range(4):
        pre = "l%d_" % li
        layers.append({k[len(pre):]: v for k, v in a.items() if k.startswith(pre)})
    return _forward(x, layers)
```

```python
import functools
import math

import jax
import jax.numpy as jnp
from jax import lax
from jax.experimental import pallas as pl
from jax.experimental.pallas import tpu as pltpu

F32 = jnp.float32
BF16 = jnp.bfloat16

V7X_VMEM_BYTES = 64 * 1024 * 1024
VMEM_LIMIT = 56 * 1024 * 1024
LANES = 128
SUBLANES = 8

RMS_EPS = 1e-6
GN_EPS = 64e-5
MOBA_HEAD_DIM = 128
MOBA_BLOCK = 256
MOBA_TOPK = 3
MOBA_PHASES = 4
SB_HEAD_DIM = 256
SB_PAIRS = 4
RWKV_HEAD = 64
LRU_BLOCKS = 16
LRU_CONV = 4
LRU_C = 8.0
FFN_CONV = 3
NEG_BIG = -1e30
LOG2E = 1.4426950408889634
LN2 = 0.6931471805599453


def _params(*sem):
    return pltpu.CompilerParams(dimension_semantics=sem, vmem_limit_bytes=VMEM_LIMIT)


def _rms(x, g):
    return x * lax.rsqrt(jnp.mean(x * x, axis=-1, keepdims=True) + RMS_EPS) * g


def _rmsnorm_kernel(x_ref, g_ref, o_ref):
    o_ref[...] = _rms(x_ref[...], g_ref[...]).astype(o_ref.dtype)


def rmsnorm(x, g, tm=256):
    s, d = x.shape
    return pl.pallas_call(
        _rmsnorm_kernel,
        grid=(s // tm,),
        in_specs=[pl.BlockSpec((tm, d), lambda i: (i, 0)),
                  pl.BlockSpec((1, d), lambda i: (0, 0))],
        out_specs=pl.BlockSpec((tm, d), lambda i: (i, 0)),
        out_shape=jax.ShapeDtypeStruct((s, d), BF16),
        compiler_params=_params("parallel"),
        name="rmsnorm",
    )(x, g.reshape(1, d))


def _add_rmsnorm_kernel(x_ref, y_ref, g1_ref, g2_ref, xo_ref, ho_ref):
    xn = x_ref[...] + _rms(y_ref[...], g1_ref[...])
    xo_ref[...] = xn
    ho_ref[...] = _rms(xn, g2_ref[...]).astype(ho_ref.dtype)


def add_rmsnorm(x, y, g_post, g_pre, tm=256):
    s, d = x.shape
    row = pl.BlockSpec((tm, d), lambda i: (i, 0))
    vec = pl.BlockSpec((1, d), lambda i: (0, 0))
    return pl.pallas_call(
        _add_rmsnorm_kernel,
        grid=(s // tm,),
        in_specs=[row, row, vec, vec],
        out_specs=[row, row],
        out_shape=[jax.ShapeDtypeStruct((s, d), F32), jax.ShapeDtypeStruct((s, d), BF16)],
        compiler_params=_params("parallel"),
        name="add_rmsnorm",
    )(x, y, g_post.reshape(1, d), g_pre.reshape(1, d))


def _mm_kernel(a_ref, b_ref, o_ref):
    o_ref[...] = jnp.dot(a_ref[...], b_ref[...], preferred_element_type=F32).astype(o_ref.dtype)


def matmul(a, b, out_dtype, tm=None, tn=512):
    m, k = a.shape
    _, n = b.shape
    if tm is None:
        tm = 1024 if k * 1024 * 2 * 2 <= VMEM_LIMIT // 3 else 512
    return pl.pallas_call(
        _mm_kernel,
        grid=(m // tm, n // tn),
        in_specs=[pl.BlockSpec((tm, k), lambda i, j: (i, 0)),
                  pl.BlockSpec((k, tn), lambda i, j: (0, j))],
        out_specs=pl.BlockSpec((tm, tn), lambda i, j: (i, j)),
        out_shape=jax.ShapeDtypeStruct((m, n), out_dtype),
        compiler_params=_params("parallel", "parallel"),
        name="matmul",
    )(a, b)


def _mm2_kernel(a1_ref, a2_ref, b1_ref, b2_ref, o_ref):
    acc = jnp.dot(a1_ref[...], b1_ref[...], preferred_element_type=F32)
    acc += jnp.dot(a2_ref[...], b2_ref[...], preferred_element_type=F32)
    o_ref[...] = acc.astype(o_ref.dtype)


def matmul_cat(a1, a2, b, out_dtype, tm=1024, tn=512):
    m, k1 = a1.shape
    _, k2 = a2.shape
    assert k1 == k2 and b.shape[0] == k1 + k2
    n = b.shape[1]
    return pl.pallas_call(
        _mm2_kernel,
        grid=(m // tm, n // tn),
        in_specs=[pl.BlockSpec((tm, k1), lambda i, j: (i, 0)),
                  pl.BlockSpec((tm, k2), lambda i, j: (i, 0)),
                  pl.BlockSpec((k1, tn), lambda i, j: (0, j)),
                  pl.BlockSpec((k2, tn), lambda i, j: (1, j))],
        out_specs=pl.BlockSpec((tm, tn), lambda i, j: (i, j)),
        out_shape=jax.ShapeDtypeStruct((m, n), out_dtype),
        compiler_params=_params("parallel", "parallel"),
        name="matmul_cat",
    )(a1, a2, b, b)


def _shift_rows(x, prev, d):
    y = pltpu.roll(x, d, 0)
    row = lax.broadcasted_iota(jnp.int32, x.shape, 0)
    for r in range(d):
        y = jnp.where(row == r, prev[SUBLANES - d + r:SUBLANES - d + r + 1, :], y)
    return y


def _ffn_up_kernel(h_ref, wg_ref, wv_ref, cg_ref, cv_ref, o_ref, hg_ref, hv_ref):
    i = pl.program_id(1)

    @pl.when(i == 0)
    def _():
        hg_ref[...] = jnp.zeros_like(hg_ref)
        hv_ref[...] = jnp.zeros_like(hv_ref)

    h = h_ref[...]

    def conv(w_ref, c_ref, halo_ref):
        raw = jnp.dot(h, w_ref[...], preferred_element_type=F32)
        prev = halo_ref[...]
        c = c_ref[...]
        out = (c[2:3, :] * raw + c[1:2, :] * _shift_rows(raw, prev, 1)
               + c[0:1, :] * _shift_rows(raw, prev, 2))
        halo_ref[...] = raw[raw.shape[0] - SUBLANES:, :]
        return out

    gate = conv(wg_ref, cg_ref, hg_ref)
    val = conv(wv_ref, cv_ref, hv_ref)
    o_ref[...] = (jax.nn.gelu(gate, approximate=True) * val).astype(o_ref.dtype)


def ffn_up(h, w_up, conv_w, tm=1024, tn=512):
    s, d = h.shape
    f = w_up.shape[1] // 2
    nj = f // tn
    return pl.pallas_call(
        _ffn_up_kernel,
        grid=(nj, s // tm),
        in_specs=[pl.BlockSpec((tm, d), lambda j, i: (i, 0)),
                  pl.BlockSpec((d, tn), lambda j, i: (0, j)),
                  pl.BlockSpec((d, tn), lambda j, i: (0, j + nj)),
                  pl.BlockSpec((FFN_CONV, tn), lambda j, i: (0, j)),
                  pl.BlockSpec((FFN_CONV, tn), lambda j, i: (0, j + nj))],
        out_specs=pl.BlockSpec((tm, tn), lambda j, i: (i, j)),
        out_shape=jax.ShapeDtypeStruct((s, f), BF16),
        scratch_shapes=[pltpu.VMEM((SUBLANES, tn), F32), pltpu.VMEM((SUBLANES, tn), F32)],
        compiler_params=_params("parallel", "arbitrary"),
        name="ffn_up",
    )(h, w_up, w_up, conv_w, conv_w)


def _dot_nt(a, b):
    return lax.dot_general(a, b, (((1,), (1,)), ((), ())), preferred_element_type=F32)


def _moba_kernel(slope_ref, q_ref, k_ref, v_ref, o_ref, kbar_ref, vt_ref,
                 s0_buf, s1_buf, p0_buf, p1_buf, *, n_blk):
    hd = pl.program_id(0)
    g = pl.program_id(1)
    bs = MOBA_BLOCK
    slope = slope_ref[hd]

    def pair_rows(ref, t):
        return ref[pl.ds(pl.multiple_of(t * (2 * bs), 2 * bs), 2 * bs), :]

    @pl.when(g == 0)
    def _():
        def kmean(j, c):
            kb = k_ref[pl.ds(pl.multiple_of(j * bs, bs), bs), :].astype(F32)
            kbar_ref[pl.ds(j, 1), :] = jnp.mean(kb, axis=0, keepdims=True)
            return c
        lax.fori_loop(0, n_blk, kmean, 0)

        def vtrans(t, c):
            vt_ref[t] = pair_rows(v_ref, t).astype(F32).T.astype(BF16)
            return c
        lax.fori_loop(0, n_blk // 2, vtrans, 0)

    q = q_ref[...]
    second = lax.broadcasted_iota(jnp.int32, (1, 2 * bs), 1) >= bs
    own = 2 * g + second.astype(jnp.int32)
    kbar = kbar_ref[...]
    kb_hi = kbar.astype(BF16)
    kb_lo = (kbar - kb_hi.astype(F32)).astype(BF16)
    gate = _dot_nt(kb_hi, q) + _dot_nt(kb_lo, q)
    blk = lax.broadcasted_iota(jnp.int32, gate.shape, 0)
    gate = jnp.where(blk < own, gate, -jnp.inf)
    sel = []
    for _ in range(MOBA_TOPK):
        m = jnp.max(gate, axis=0, keepdims=True)
        cand = (gate == m) & (gate > -jnp.inf)
        idx = jnp.min(jnp.where(cand, blk, n_blk), axis=0, keepdims=True)
        sel.append(idx)
        gate = jnp.where(blk == idx, -jnp.inf, gate)

    slope2 = slope * LOG2E
    key = lax.broadcasted_iota(jnp.int32, (2 * bs, 2 * bs), 0)
    qry = lax.broadcasted_iota(jnp.int32, (2 * bs, 2 * bs), 1)
    cb_pair = slope2 * key.astype(F32)

    def qbias(j):
        picked = (sel[0] == j) | (sel[1] == j) | (sel[2] == j)
        return jnp.where(picked, 0.0, NEG_BIG)

    s = _dot_nt(pair_rows(k_ref, g), q) + cb_pair
    ok_first = jnp.where(second, qbias(2 * g), 0.0)
    s = jnp.where(key <= qry, s, NEG_BIG)
    s_a, s_b = s[:bs] + ok_first, s[bs:]
    m0 = jnp.maximum(jnp.max(s_a, axis=0, keepdims=True), jnp.max(s_b, axis=0, keepdims=True))
    p = jnp.exp2(jnp.concatenate([s_a, s_b], axis=0) - m0)
    l0 = jnp.sum(p, axis=0, keepdims=True)
    acc0 = jnp.dot(vt_ref[g], p.astype(BF16), preferred_element_type=F32)

    n_it = g
    t_last = jnp.maximum(n_it - 1, 0)

    s_buf = (s0_buf, s1_buf)
    p_buf = (p0_buf, p1_buf)
    s_buf[0][...] = _dot_nt(pair_rows(k_ref, 0), q)
    p_buf[1][...] = jnp.zeros((2 * bs, 2 * bs), BF16)

    def phase(t, carry, cur):
        m_i, l_i, acc = carry
        s_buf[1 - cur][...] = _dot_nt(pair_rows(k_ref, jnp.minimum(t + 1, t_last)), q)
        pv = jnp.dot(vt_ref[jnp.clip(t - 1, 0, t_last)], p_buf[1 - cur][...],
                     preferred_element_type=F32)
        ja = 2 * t
        s = s_buf[cur][...] + cb_pair
        sa, sb = s[:bs], s[bs:]
        off = jnp.where(t < n_it, slope2 * ((ja - 2 * g) * bs).astype(F32), NEG_BIG)
        qa, qb = qbias(ja) + off, qbias(ja + 1) + off
        m_new = jnp.maximum(m_i, jnp.maximum(jnp.max(sa, axis=0, keepdims=True) + qa,
                                             jnp.max(sb, axis=0, keepdims=True) + qb))
        pa = jnp.exp2(sa + (qa - m_new))
        pb = jnp.exp2(sb + (qb - m_new))
        alpha = jnp.exp2(m_i - m_new)
        l_new = (alpha * l_i + jnp.sum(pa, axis=0, keepdims=True)
                 + jnp.sum(pb, axis=0, keepdims=True))
        p_buf[cur][...] = jnp.concatenate([pa, pb], axis=0).astype(BF16)
        return m_new, l_new, alpha * (acc + pv)

    def body(u, carry):
        for ph in range(MOBA_PHASES):
            carry = phase(MOBA_PHASES * u + ph, carry, ph % 2)
        return carry

    n_trip = (n_it + MOBA_PHASES - 1) // MOBA_PHASES
    _, l_f, acc_f = lax.fori_loop(0, n_trip, body, (m0, l0, acc0))
    acc_f = acc_f + jnp.dot(vt_ref[jnp.clip(MOBA_PHASES * n_trip - 1, 0, t_last)], p_buf[1][...],
                            preferred_element_type=F32)
    o_ref[...] = (acc_f / l_f).T.astype(o_ref.dtype)


def moba_attention(proj, n_heads):
    s = proj.shape[0]
    dh, bs = MOBA_HEAD_DIM, MOBA_BLOCK
    n_blk = s // bs
    slopes = 2.0 ** (-8.0 * jnp.arange(1, n_heads + 1, dtype=F32) / n_heads)
    grid_spec = pltpu.PrefetchScalarGridSpec(
        num_scalar_prefetch=1,
        grid=(n_heads, n_blk // 2),
        in_specs=[pl.BlockSpec((2 * bs, dh), lambda h, g, sl: (g, h)),
                  pl.BlockSpec((s, dh), lambda h, g, sl: (0, n_heads + h)),
                  pl.BlockSpec((s, dh), lambda h, g, sl: (0, 2 * n_heads + h))],
        out_specs=pl.BlockSpec((2 * bs, dh), lambda h, g, sl: (g, h)),
        scratch_shapes=[pltpu.VMEM((n_blk, dh), F32),
                        pltpu.VMEM((n_blk // 2, dh, 2 * bs), BF16),
                        pltpu.VMEM((2 * bs, 2 * bs), F32), pltpu.VMEM((2 * bs, 2 * bs), F32),
                        pltpu.VMEM((2 * bs, 2 * bs), BF16), pltpu.VMEM((2 * bs, 2 * bs), BF16)],
    )
    return pl.pallas_call(
        functools.partial(_moba_kernel, n_blk=n_blk),
        grid_spec=grid_spec,
        out_shape=jax.ShapeDtypeStruct((s, n_heads * dh), BF16),
        compiler_params=_params("parallel", "arbitrary"),
        name="moba",
    )(slopes, proj, proj, proj)


def _softplus(z):
    return jnp.maximum(z, 0.0) + jnp.log1p(jnp.exp(-jnp.abs(z)))


def _sb_kernel(q_ref, k_ref, v_ref, o_ref, z0_buf, z1_buf, w0_buf, w1_buf, *, tb):
    i = pl.program_id(1)
    q = q_ref[...]
    row = lax.broadcasted_iota(jnp.int32, (tb, tb), 0)
    col = lax.broadcasted_iota(jnp.int32, (tb, tb), 1)
    tri = jnp.where(row >= col, 1.0, 0.0).astype(BF16)
    past = col < row

    def weights(z2, later, masked):
        nz = -z2
        lk = jnp.minimum(nz, 0.0) - jnp.log2(1.0 + jnp.exp2(jnp.minimum(z2, nz)))
        if masked:
            lk = jnp.where(past, lk, 0.0)
        c = jnp.dot(lk.astype(BF16), tri, preferred_element_type=F32)
        w = jnp.exp2(z2 + c + later)
        if masked:
            w = jnp.where(past, w, 0.0)
        return w.astype(BF16), jnp.sum(lk, axis=-1, keepdims=True)

    i0 = pl.multiple_of(i * tb, tb)
    w0, later0 = weights(_dot_nt(q, k_ref[pl.ds(i0, tb), :]), jnp.zeros((tb, 1), F32), True)
    acc0 = jnp.dot(w0, v_ref[pl.ds(i0, tb), :], preferred_element_type=F32)

    n_pair = i // 2
    n_trip = (n_pair + SB_PAIRS - 1) // SB_PAIRS
    t_last = jnp.maximum(n_pair - 1, 0)
    z_buf = (z0_buf, z1_buf)
    w_buf = (w0_buf, w1_buf)

    def pair_rows(ref, t):
        j0 = jnp.maximum(i - 2 - 2 * t, 0) * tb
        return ref[pl.ds(pl.multiple_of(j0, tb), 2 * tb), :]

    z_buf[0][...] = _dot_nt(q, pair_rows(k_ref, 0))
    w_buf[1][...] = jnp.zeros((tb, 2 * tb), BF16)

    def pair(t, carry, cur):
        acc, later = carry
        z_buf[1 - cur][...] = _dot_nt(q, pair_rows(k_ref, jnp.minimum(t + 1, t_last)))
        acc = acc + jnp.dot(w_buf[1 - cur][...], pair_rows(v_ref, jnp.clip(t - 1, 0, t_last)),
                            preferred_element_type=F32)
        z2 = z_buf[cur][...]
        valid = t < n_pair
        later_hi = jnp.where(valid, later, NEG_BIG)
        w_hi, tot_hi = weights(z2[:, tb:], later_hi, False)
        w_lo, tot_lo = weights(z2[:, :tb], later_hi + tot_hi, False)
        w_buf[cur][...] = jnp.concatenate([w_lo, w_hi], axis=1)
        return acc, later + jnp.where(valid, tot_hi + tot_lo, 0.0)

    def trip(u, carry):
        for ph in range(SB_PAIRS):
            carry = pair(SB_PAIRS * u + ph, carry, ph % 2)
        return carry

    acc, later = lax.fori_loop(0, n_trip, trip, (acc0, later0))
    acc = acc + jnp.dot(w_buf[1][...], pair_rows(v_ref, jnp.clip(SB_PAIRS * n_trip - 1, 0, t_last)),
                        preferred_element_type=F32)

    def single(t, carry):
        acc, later = carry
        w, tot = weights(_dot_nt(q, k_ref[pl.ds(0, tb), :]), later, False)
        return acc + jnp.dot(w, v_ref[pl.ds(0, tb), :], preferred_element_type=F32), later + tot

    acc, _ = lax.fori_loop(0, i - 2 * n_pair, single, (acc, later))
    o_ref[...] = acc.astype(o_ref.dtype)


def sb_attention(proj, n_heads, col0, tb=256):
    s = proj.shape[0]
    dh = SB_HEAD_DIM
    c0 = col0 // dh
    return pl.pallas_call(
        functools.partial(_sb_kernel, tb=tb),
        grid=(n_heads, s // tb),
        in_specs=[pl.BlockSpec((tb, dh), lambda h, i: (i, c0 + h)),
                  pl.BlockSpec((s, dh), lambda h, i: (0, c0 + n_heads + h)),
                  pl.BlockSpec((s, dh), lambda h, i: (0, c0 + 2 * n_heads + h))],
        out_specs=pl.BlockSpec((tb, dh), lambda h, i: (i, h)),
        out_shape=jax.ShapeDtypeStruct((s, n_heads * dh), BF16),
        scratch_shapes=[pltpu.VMEM((tb, 2 * tb), F32), pltpu.VMEM((tb, 2 * tb), F32),
                        pltpu.VMEM((tb, 2 * tb), BF16), pltpu.VMEM((tb, 2 * tb), BF16)],
        compiler_params=_params("parallel", "arbitrary"),
        name="stickbreak",
    )(proj, proj, proj)


RWKV_W = 2048
LORA_W = 640
HEAD_GROUP = 256
RWKV_CHUNK = 64


def _sigmoid(x):
    return 1.0 / (1.0 + jnp.exp(-x))


def _head_ones(n):
    r = lax.broadcasted_iota(jnp.int32, (n, n), 0) // RWKV_HEAD
    c = lax.broadcasted_iota(jnp.int32, (n, n), 1) // RWKV_HEAD
    return jnp.where(r == c, 1.0, 0.0).astype(BF16)


def _head_sum(x):
    e = _head_ones(HEAD_GROUP)
    hi = x.astype(BF16)
    lo = (x - hi.astype(F32)).astype(BF16)
    outs = []
    for c in range(x.shape[1] // HEAD_GROUP):
        sl = slice(c * HEAD_GROUP, (c + 1) * HEAD_GROUP)
        outs.append(jnp.dot(hi[:, sl], e, preferred_element_type=F32)
                    + jnp.dot(lo[:, sl], e, preferred_element_type=F32))
    return jnp.concatenate(outs, axis=-1)


def _token_shift(x_ref, halo_ref, mu_ref, first):
    x = x_ref[...]
    prev = jnp.where(first, 0.0, halo_ref[...])
    return x + (_shift_rows(x, prev, 1) - x) * mu_ref[...]


def _rwkv_prep_kernel(*refs, vres):
    (r_ref, k_ref, v_ref, lo_ref, rh_ref, kh_ref, vh_ref, loh_ref,
     mur_ref, muk_ref, muv_ref, mulo_ref,
     w0_ref, w2_ref, a0_ref, a2_ref, v0_ref, v2_ref, g2_ref, kk_ref, ka_ref, rk_ref) = refs[:22]
    rest = refs[22:]
    if vres:
        vf_ref, rest = rest[0], rest[1:]
    ro_ref, lw_ref, ko_ref, vo_ref, na_ref, bb_ref, g_ref, bo_ref = rest
    first = pl.program_id(0) == 0
    r = _token_shift(r_ref, rh_ref, mur_ref, first)
    k = _token_shift(k_ref, kh_ref, muk_ref, first)
    v = _token_shift(v_ref, vh_ref, muv_ref, first)
    lo = _token_shift(lo_ref, loh_ref, mulo_ref, first)
    w_lo, a_lo, v_lo, g_lo = lo[:, 0:128], lo[:, 128:256], lo[:, 256:384], lo[:, 384:640]

    def lora(x, w_ref):
        return jnp.dot(x.astype(BF16), w_ref[...], preferred_element_type=F32)

    w_log = -_softplus(-(w0_ref[...] + lora(jnp.tanh(w_lo), w2_ref))) - 0.5
    lw_ref[...] = -jnp.exp(w_log)
    if vres:
        v = v + (vf_ref[...] - v) * _sigmoid(v0_ref[...] + lora(v_lo, v2_ref))
    a = _sigmoid(a0_ref[...] + lora(a_lo, a2_ref))
    g_ref[...] = lora(_sigmoid(g_lo), g2_ref)
    kk = k * kk_ref[...]
    kk = kk / jnp.maximum(jnp.sqrt(_head_sum(kk * kk)), 1e-12)
    k2 = k * (1.0 + (a - 1.0) * ka_ref[...])
    ro_ref[...] = r
    ko_ref[...] = k2
    vo_ref[...] = v
    na_ref[...] = -kk
    bb_ref[...] = kk * a
    bo_ref[...] = _head_sum(r * k2 * rk_ref[...]) * v


def rwkv_prep(proj, col0, mu, w0, w2p, a0, a2p, v0, v2p, g2, k_k, k_a, r_k, v_first, tm=128):
    s = proj.shape[0]
    w = RWKV_W
    vres = v_first is not None
    c_r, c_lo = col0 // w, (col0 + 3 * w) // LORA_W
    hb = tm // SUBLANES

    def main(width, cidx):
        return pl.BlockSpec((tm, width), lambda i: (i, cidx))

    def halo(width, cidx):
        return pl.BlockSpec((SUBLANES, width), lambda i: (jnp.maximum(i * hb - 1, 0), cidx))

    def vec(width, cidx=0):
        return pl.BlockSpec((1, width), lambda i: (0, cidx))

    def full(shape):
        return pl.BlockSpec(shape, lambda i: (0, 0))

    mu2 = mu.reshape(1, -1)
    r1 = lambda t: t.reshape(1, -1)
    in_specs = [main(w, c_r), main(w, c_r + 1), main(w, c_r + 2), main(LORA_W, c_lo),
                halo(w, c_r), halo(w, c_r + 1), halo(w, c_r + 2), halo(LORA_W, c_lo),
                vec(w, c_r), vec(w, c_r + 1), vec(w, c_r + 2), vec(LORA_W, c_lo),
                vec(w), full((128, w)), vec(w), full((128, w)), vec(w), full((128, w)),
                full((256, w)), vec(w), vec(w), vec(w)]
    args = [proj] * 8 + [mu2] * 4 + [r1(w0), w2p, r1(a0), a2p, r1(v0), v2p, g2, r1(k_k), r1(k_a), r1(r_k)]
    if vres:
        in_specs.append(main(w, 0))
        args.append(v_first)
    out = pl.BlockSpec((tm, w), lambda i: (i, 0))
    return pl.pallas_call(
        functools.partial(_rwkv_prep_kernel, vres=vres),
        grid=(s // tm,),
        in_specs=in_specs,
        out_specs=[out] * 8,
        out_shape=[jax.ShapeDtypeStruct((s, w), F32)] * 8,
        compiler_params=_params("parallel"),
        name="rwkv_prep",
    )(*args)


def _cumsum_rows(x):
    n = x.shape[0]
    row = lax.broadcasted_iota(jnp.int32, x.shape, 0)
    d = 1
    while d < n:
        x = x + jnp.where(row >= d, pltpu.roll(x, d, 0), 0.0)
        d *= 2
    return x


def _rwkv_chunk(r, lw, k, v, a, b, state):
    L, n2 = r.shape
    nh = n2 // RWKV_HEAD
    J = nh * L
    bd = lambda x: x.astype(BF16)
    mm = lambda x, y: jnp.dot(bd(x), bd(y), preferred_element_type=F32)
    lg = _cumsum_rows(lw)
    tot = lg[L - 1:L, :]
    e_neg = jnp.exp(-lg)
    e_hat = jnp.exp(tot - lg)
    a_t = a * jnp.exp(lg - lw)
    r_t = r * jnp.exp(lg)
    b_i, k_i = b * e_neg, k * e_neg
    b_hat, k_hat = b * e_hat, k * e_hat

    head = lax.broadcasted_iota(jnp.int32, (L, n2), 1) // RWKV_HEAD

    def stack(x):
        return jnp.concatenate([jnp.where(head == h, x, 0.0) for h in range(nh)], axis=0)

    def fold(x):
        out = x[:L]
        for h in range(1, nh):
            out = out + x[h * L:(h + 1) * L]
        return out

    row = lax.broadcasted_iota(jnp.int32, (J, J), 0)
    col = lax.broadcasted_iota(jnp.int32, (J, J), 1)
    same = (row // L) == (col // L)
    strict = same & (col < row)
    incl = same & (col <= row)
    eye = jnp.where(row == col, 1.0, 0.0)

    x_a, x_r = stack(a_t), stack(r_t)
    ybk = jnp.concatenate([b_i] * nh + [k_i] * nh, axis=0)
    pa = _dot_nt(bd(x_a), bd(ybk))
    pr = _dot_nt(bd(x_r), bd(ybk))
    yield
    m_ab = jnp.where(strict, pa[:, :J], 0.0)
    m_ak = jnp.where(strict, pa[:, J:], 0.0)
    m_rb = jnp.where(incl, pr[:, :J], 0.0)
    m_rk = jnp.where(incl, pr[:, J:], 0.0)

    v_st = stack(v)
    w1 = mm(m_ak, v_st)
    t_inv, pw = eye + m_ab, m_ab
    for _ in range(int(math.log2(L)) - 1):
        pw = mm(pw, pw)
        yield
        t_inv = mm(t_inv, eye + pw)

    yield
    tw = mm(t_inv, jnp.concatenate([x_a, w1], axis=1))
    yield
    ta, w0 = fold(tw[:, :n2]), fold(tw[:, n2:])
    gs = _dot_nt(bd(jnp.concatenate([ta, r_t], axis=0)), bd(state))
    lhs_t = jnp.concatenate([jnp.concatenate([ta, jnp.zeros_like(ta)], axis=0).T,
                             jnp.concatenate([w0, v], axis=0).T], axis=0)
    pq = mm(lhs_t, jnp.concatenate([b_hat, k_hat], axis=0))
    yield
    u = gs[:L] + w0
    yj = mm(jnp.concatenate([m_rb, m_rk], axis=1), jnp.concatenate([stack(u), v_st], axis=0))
    y = gs[L:] + fold(yj)
    yield
    ch_r = lax.broadcasted_iota(jnp.int32, (n2, n2), 0)
    ch_c = lax.broadcasted_iota(jnp.int32, (n2, n2), 1)
    head_blk = (ch_r // RWKV_HEAD) == (ch_c // RWKV_HEAD)
    p_mat = jnp.where(head_blk, pq[:n2], 0.0) + jnp.where(ch_r == ch_c, jnp.exp(tot), 0.0)
    q_mat = jnp.where(head_blk, pq[n2:], 0.0)
    return y, mm(state, p_mat) + q_mat


def _run_lockstep(chains):
    results = [None] * len(chains)
    live = list(range(len(chains)))
    while live:
        for g in list(live):
            try:
                next(chains[g])
            except StopIteration as done:
                results[g] = done.value
                live.remove(g)
    return results


def _rwkv_rec_kernel(r_ref, lw_ref, k_ref, v_ref, a_ref, b_ref, y_ref, st_ref, *, n_groups, n_chunks):
    gw = st_ref.shape[-1]

    @pl.when(pl.program_id(1) == 0)
    def _():
        st_ref[...] = jnp.zeros_like(st_ref)

    def body(c, carry):
        rows = pl.ds(pl.multiple_of(c * RWKV_CHUNK, RWKV_CHUNK), RWKV_CHUNK)
        in_refs = (r_ref, lw_ref, k_ref, v_ref, a_ref, b_ref)
        chains = [_rwkv_chunk(*[ref[rows, p * gw:(p + 1) * gw] for ref in in_refs], st_ref[p])
                  for p in range(n_groups)]
        for p, (y, st) in enumerate(_run_lockstep(chains)):
            y_ref[rows, p * gw:(p + 1) * gw] = y
            st_ref[p] = st
        return carry

    lax.fori_loop(0, n_chunks, body, 0)


def rwkv_recurrence(r, lw, k, v, a, b, tt=512, n_groups=4, heads_per_group=4):
    s, w = r.shape
    gw = heads_per_group * RWKV_HEAD
    wb = n_groups * gw
    blk = pl.BlockSpec((tt, wb), lambda h, t: (t, h))
    return pl.pallas_call(
        functools.partial(_rwkv_rec_kernel, n_groups=n_groups, n_chunks=tt // RWKV_CHUNK),
        grid=(w // wb, s // tt),
        in_specs=[blk] * 6,
        out_specs=blk,
        out_shape=jax.ShapeDtypeStruct((s, w), F32),
        scratch_shapes=[pltpu.VMEM((n_groups, gw, gw), F32)],
        compiler_params=_params("arbitrary", "arbitrary"),
        name="rwkv_rec",
    )(r, lw, k, v, a, b)


def _rwkv_post_kernel(y_ref, g_ref, bo_ref, lw_ref, lb_ref, o_ref):
    y = y_ref[...]
    inv_n = 1.0 / RWKV_HEAD
    mu = _head_sum(y) * inv_n
    yc = y - mu
    var = _head_sum(yc * yc) * inv_n
    yn = yc * lax.rsqrt(var + GN_EPS) * lw_ref[...] + lb_ref[...]
    o_ref[...] = ((yn + bo_ref[...]) * g_ref[...]).astype(o_ref.dtype)


def rwkv_post(y, g, bonus, lnx_w, lnx_b, tm=256):
    s, w = y.shape
    row = pl.BlockSpec((tm, w), lambda i: (i, 0))
    vec = pl.BlockSpec((1, w), lambda i: (0, 0))
    return pl.pallas_call(
        _rwkv_post_kernel,
        grid=(s // tm,),
        in_specs=[row, row, row, vec, vec],
        out_specs=row,
        out_shape=jax.ShapeDtypeStruct((s, w), BF16),
        compiler_params=_params("parallel"),
        name="rwkv_post",
    )(y, g, bonus, lnx_w.reshape(1, w), lnx_b.reshape(1, w))


def _expm1(y):
    u = jnp.exp(y)
    near = (u - 1.0) * y / jnp.log(u)
    return jnp.where(u == 1.0, y, jnp.where(y < -0.5, u - 1.0, near))


def _lru_kernel(gate_ref, x_ref, xh_ref, cw_ref, cb_ref, gw_ref, gb_ref, lam_ref, o_ref, h_ref, *, nb):
    t = pl.program_id(1)

    @pl.when(t == 0)
    def _():
        h_ref[...] = jnp.zeros_like(h_ref)

    x = x_ref[...]
    prev = jnp.where(t == 0, 0.0, xh_ref[...])
    cw = cw_ref[...]
    xb = cw[3:4, :] * x + cb_ref[...]
    for d in range(1, LRU_CONV):
        xb = xb + cw[3 - d:4 - d, :] * _shift_rows(x, prev, d)
    xb16 = xb.astype(BF16)
    g0, g1 = [], []
    for n in range(nb):
        xs = xb16[:, n * LANES:(n + 1) * LANES]
        g0.append(jnp.dot(xs, gw_ref[0, n], preferred_element_type=F32))
        g1.append(jnp.dot(xs, gw_ref[1, n], preferred_element_type=F32))
    gb = gb_ref[...]
    g0 = jnp.concatenate(g0, axis=-1) + gb[0:1, :]
    g1 = jnp.concatenate(g1, axis=-1) + gb[1:2, :]
    log_a = -LRU_C * _sigmoid(g0) * _softplus(-lam_ref[...])
    a = jnp.exp(log_a)
    b = jnp.sqrt(-_expm1(2.0 * log_a)) * _sigmoid(g1) * xb
    n = x.shape[0]
    row = lax.broadcasted_iota(jnp.int32, x.shape, 0)
    d = 1
    while d < n:
        keep = row >= d
        a_s = jnp.where(keep, pltpu.roll(a, d, 0), 1.0)
        b_s = jnp.where(keep, pltpu.roll(b, d, 0), 0.0)
        b = a * b_s + b
        a = a * a_s
        d *= 2
    h = b + a * h_ref[SUBLANES - 1:SUBLANES, :]
    h_ref[...] = h[n - SUBLANES:, :]
    o_ref[...] = (h * jax.nn.gelu(gate_ref[...], approximate=True)).astype(o_ref.dtype)


def rg_lru(proj, conv_w, conv_b, gate_w, gate_b, lam, tm=256, tc=512):
    s = proj.shape[0]
    w = conv_w.shape[1]
    nj = w // tc
    nb = tc // LANES
    hb = tm // SUBLANES
    return pl.pallas_call(
        functools.partial(_lru_kernel, nb=nb),
        grid=(nj, s // tm),
        in_specs=[pl.BlockSpec((tm, tc), lambda j, t: (t, j)),
                  pl.BlockSpec((tm, tc), lambda j, t: (t, nj + j)),
                  pl.BlockSpec((SUBLANES, tc), lambda j, t: (jnp.maximum(t * hb - 1, 0), nj + j)),
                  pl.BlockSpec((LRU_CONV, tc), lambda j, t: (0, j)),
                  pl.BlockSpec((1, tc), lambda j, t: (0, j)),
                  pl.BlockSpec((2, nb, LANES, LANES), lambda j, t: (0, j, 0, 0)),
                  pl.BlockSpec((2, tc), lambda j, t: (0, j)),
                  pl.BlockSpec((1, tc), lambda j, t: (0, j))],
        out_specs=pl.BlockSpec((tm, tc), lambda j, t: (t, j)),
        out_shape=jax.ShapeDtypeStruct((s, w), BF16),
        scratch_shapes=[pltpu.VMEM((SUBLANES, tc), F32)],
        compiler_params=_params("arbitrary", "arbitrary"),
        name="rg_lru",
    )(proj, proj, proj, conv_w, conv_b.reshape(1, w), gate_w.astype(BF16), gate_b, lam.reshape(1, w))


def even_mixer(h, w_in, w_out):
    half = w_out.shape[0] // 2
    n_moba = half // MOBA_HEAD_DIM
    n_sb = half // SB_HEAD_DIM
    col = jnp.arange(w_in.shape[1])
    qscale = jnp.where(col < half, MOBA_HEAD_DIM ** -0.5 * LOG2E,
                       jnp.where((col >= 3 * half) & (col < 4 * half), SB_HEAD_DIM ** -0.5 * LOG2E, 1.0))
    proj = matmul(h, (w_in * qscale.astype(F32)).astype(BF16), BF16)
    oa = moba_attention(proj, n_moba)
    ob = sb_attention(proj, n_sb, 3 * n_moba * MOBA_HEAD_DIM)
    return matmul_cat(oa, ob, w_out.astype(BF16), F32)


def _pad_cols(t, width):
    return jnp.pad(t, [(0, 0)] * (t.ndim - 1) + [(0, width - t.shape[-1])])


def _odd_layout(t, n_c, vres):
    w = RWKV_W
    sizes = [96, 96] + ([64] if vres else []) + [256]
    segs, c = [], 3 * w
    for sz in sizes:
        segs.append(_pad_cols(t[..., c:c + sz], LANES if sz < LANES else sz))
        c += sz
    if not vres:
        segs.insert(2, jnp.zeros(t.shape[:-1] + (LANES,), t.dtype))
    assert c == n_c
    return jnp.concatenate([t[..., n_c:], t[..., :3 * w]] + segs, axis=-1)


def _pad_rows(t, rows):
    return jnp.pad(t, ((0, rows - t.shape[0]), (0, 0)))


def odd_mixer(h, v_first, p):
    w = RWKV_W
    vres = "v2" in p
    n_c = p["shift_mu"].shape[0]
    w_in = _odd_layout(p["w_in"], n_c, vres).astype(BF16)
    lru_w = p["w_in"].shape[1] - n_c
    mu = _odd_layout(jnp.concatenate([p["shift_mu"], jnp.zeros((lru_w,), F32)]), n_c, vres)
    proj = matmul(h, w_in, F32, tn=LORA_W)
    zeros_w = jnp.zeros((w,), F32)
    r, lw, k, v, na, bb, g, bonus = rwkv_prep(
        proj, lru_w, mu, p["w0"], _pad_rows(p["w2"], LANES).astype(BF16),
        p["a0"], _pad_rows(p["a2"], LANES).astype(BF16),
        p["v0"] if vres else zeros_w,
        _pad_rows(p["v2"], LANES).astype(BF16) if vres else jnp.zeros((LANES, w), BF16),
        p["g2"].astype(BF16), p["k_k"], p["k_a"], p["r_k"].reshape(-1), v_first)
    if v_first is None:
        v_first = v
    y = rwkv_recurrence(r, lw, k, v, na, bb)
    c_out = rwkv_post(y, g, bonus, p["lnx_w"], p["lnx_b"])
    d_out = rg_lru(proj, p["conv_w"], p["conv_b"], p["gate_w"], p["gate_b"], p["lru_lambda"])
    return matmul_cat(c_out, d_out, p["w_out"].astype(BF16), F32), v_first


_ODD_NAMES = ("w_in", "w_out", "shift_mu", "w0", "w2", "a0", "a2", "g2", "k_k", "k_a", "r_k",
              "lnx_w", "lnx_b", "conv_w", "conv_b", "gate_w", "gate_b", "lru_lambda")


def _forward(x, layers):
    x2 = x.reshape(x.shape[-2], x.shape[-1])
    h = rmsnorm(x2, layers[0]["norms"][0])
    v_first = None
    for li, p in enumerate(layers):
        g = p["norms"]
        if "shift_mu" in p:
            mix, v_first = odd_mixer(h, v_first, p)
        else:
            mix = even_mixer(h, p["w_in"], p["w_out"])
        x2, h = add_rmsnorm(x2, mix, g[1], g[2])
        act = ffn_up(h, p["ffn_up"].astype(BF16), p["ffn_conv"])
        y = matmul(act, p["ffn_down"].astype(BF16), F32)
        g_next = layers[li + 1]["norms"][0] if li + 1 < len(layers) else g[3]
        x2, h = add_rmsnorm(x2, y, g[3], g_next)
    return x2.reshape(x.shape)


def kernel(x, l0_norms, l0_w_in, l0_w_out, l0_ffn_up, l0_ffn_conv, l0_ffn_down, l1_norms, l1_w_in, l1_w_out, l1_shift_mu, l1_w0, l1_w2, l1_a0, l1_a2, l1_g2, l1_k_k, l1_k_a, l1_r_k, l1_lnx_w, l1_lnx_b, l1_conv_w, l1_conv_b, l1_gate_w, l1_gate_b, l1_lru_lambda, l1_ffn_up, l1_ffn_conv, l1_ffn_down, l2_norms, l2_w_in, l2_w_out, l2_ffn_up, l2_ffn_conv, l2_ffn_down, l3_norms, l3_w_in, l3_w_out, l3_shift_mu, l3_w0, l3_w2, l3_a0, l3_a2, l3_v0, l3_v2, l3_g2, l3_k_k, l3_k_a, l3_r_k, l3_lnx_w, l3_lnx_b, l3_conv_w, l3_conv_b, l3_gate_w, l3_gate_b, l3_lru_lambda, l3_ffn_up, l3_ffn_conv, l3_ffn_down):
    a = locals()
    layers = []
    for li in range(4):
        pre = "l%d_" % li
        layers.append({k[len(pre):]: v for k, v in a.items() if k.startswith(pre)})
    return _forward(x, layers)
```

```python
import functools
import math

import jax
import jax.numpy as jnp
from jax import lax
from jax.experimental import pallas as pl
from jax.experimental.pallas import tpu as pltpu

F32 = jnp.float32
BF16 = jnp.bfloat16

V7X_VMEM_BYTES = 64 * 1024 * 1024
VMEM_LIMIT = V7X_VMEM_BYTES - 8 * 1024 * 1024
LANES = 128
SUBLANES = 8

RMS_EPS = 1e-6
GN_EPS = 64e-5
MOBA_HEAD_DIM = 128
MOBA_BLOCK = 256
MOBA_TOPK = 3
MOBA_PHASES = 4
SB_HEAD_DIM = 256
SB_PAIRS = 4
RWKV_HEAD = 64
LRU_BLOCKS = 16
LRU_CONV = 4
LRU_C = 8.0
FFN_CONV = 3
NEG_BIG = -1e30
LOG2E = 1.4426950408889634


def _params(*sem):
    return pltpu.CompilerParams(dimension_semantics=sem, vmem_limit_bytes=VMEM_LIMIT)


def _rms(x, g):
    return x * lax.rsqrt(jnp.mean(x * x, axis=-1, keepdims=True) + RMS_EPS) * g


def _rmsnorm_kernel(x_ref, g_ref, o_ref):
    o_ref[...] = _rms(x_ref[...], g_ref[...]).astype(o_ref.dtype)


def rmsnorm(x, g, tm=256):
    s, d = x.shape
    return pl.pallas_call(
        _rmsnorm_kernel,
        grid=(s // tm,),
        in_specs=[pl.BlockSpec((tm, d), lambda i: (i, 0)),
                  pl.BlockSpec((1, d), lambda i: (0, 0))],
        out_specs=pl.BlockSpec((tm, d), lambda i: (i, 0)),
        out_shape=jax.ShapeDtypeStruct((s, d), BF16),
        compiler_params=_params("parallel"),
        name="rmsnorm",
    )(x, g.reshape(1, d))


def _add_rmsnorm_kernel(x_ref, y_ref, g1_ref, g2_ref, xo_ref, ho_ref):
    xn = x_ref[...] + _rms(y_ref[...], g1_ref[...])
    xo_ref[...] = xn
    ho_ref[...] = _rms(xn, g2_ref[...]).astype(ho_ref.dtype)


def add_rmsnorm(x, y, g_post, g_pre, tm=256):
    s, d = x.shape
    row = pl.BlockSpec((tm, d), lambda i: (i, 0))
    vec = pl.BlockSpec((1, d), lambda i: (0, 0))
    return pl.pallas_call(
        _add_rmsnorm_kernel,
        grid=(s // tm,),
        in_specs=[row, row, vec, vec],
        out_specs=[row, row],
        out_shape=[jax.ShapeDtypeStruct((s, d), F32), jax.ShapeDtypeStruct((s, d), BF16)],
        compiler_params=_params("parallel"),
        name="add_rmsnorm",
    )(x, y, g_post.reshape(1, d), g_pre.reshape(1, d))


def _mm_kernel(a_ref, b_ref, o_ref):
    o_ref[...] = jnp.dot(a_ref[...], b_ref[...], preferred_element_type=F32).astype(o_ref.dtype)


def matmul(a, b, out_dtype, tm=None, tn=512):
    m, k = a.shape
    _, n = b.shape
    if tm is None:
        tm = 1024 if k * 1024 * 2 * 2 <= VMEM_LIMIT // 3 else 512
    return pl.pallas_call(
        _mm_kernel,
        grid=(m // tm, n // tn),
        in_specs=[pl.BlockSpec((tm, k), lambda i, j: (i, 0)),
                  pl.BlockSpec((k, tn), lambda i, j: (0, j))],
        out_specs=pl.BlockSpec((tm, tn), lambda i, j: (i, j)),
        out_shape=jax.ShapeDtypeStruct((m, n), out_dtype),
        compiler_params=_params("parallel", "parallel"),
        name="matmul",
    )(a, b)


def _mm2_kernel(a1_ref, a2_ref, b1_ref, b2_ref, o_ref):
    acc = jnp.dot(a1_ref[...], b1_ref[...], preferred_element_type=F32)
    acc += jnp.dot(a2_ref[...], b2_ref[...], preferred_element_type=F32)
    o_ref[...] = acc.astype(o_ref.dtype)


def matmul_cat(a1, a2, b, out_dtype, tm=1024, tn=512):
    m, k1 = a1.shape
    _, k2 = a2.shape
    assert k1 == k2 and b.shape[0] == k1 + k2
    n = b.shape[1]
    return pl.pallas_call(
        _mm2_kernel,
        grid=(m // tm, n // tn),
        in_specs=[pl.BlockSpec((tm, k1), lambda i, j: (i, 0)),
                  pl.BlockSpec((tm, k2), lambda i, j: (i, 0)),
                  pl.BlockSpec((k1, tn), lambda i, j: (0, j)),
                  pl.BlockSpec((k2, tn), lambda i, j: (1, j))],
        out_specs=pl.BlockSpec((tm, tn), lambda i, j: (i, j)),
        out_shape=jax.ShapeDtypeStruct((m, n), out_dtype),
        compiler_params=_params("parallel", "parallel"),
        name="matmul_cat",
    )(a1, a2, b, b)


def _shift_rows(x, prev, d):
    y = pltpu.roll(x, d, 0)
    row = lax.broadcasted_iota(jnp.int32, x.shape, 0)
    for r in range(d):
        y = jnp.where(row == r, prev[SUBLANES - d + r:SUBLANES - d + r + 1, :], y)
    return y


def _ffn_up_kernel(h_ref, wg_ref, wv_ref, cg_ref, cv_ref, o_ref, hg_ref, hv_ref):
    i = pl.program_id(1)

    @pl.when(i == 0)
    def _():
        hg_ref[...] = jnp.zeros_like(hg_ref)
        hv_ref[...] = jnp.zeros_like(hv_ref)

    h = h_ref[...]

    def conv(w_ref, c_ref, halo_ref):
        raw = jnp.dot(h, w_ref[...], preferred_element_type=F32)
        prev = halo_ref[...]
        c = c_ref[...]
        out = (c[2:3, :] * raw + c[1:2, :] * _shift_rows(raw, prev, 1)
               + c[0:1, :] * _shift_rows(raw, prev, 2))
        halo_ref[...] = raw[raw.shape[0] - SUBLANES:, :]
        return out

    gate = conv(wg_ref, cg_ref, hg_ref)
    val = conv(wv_ref, cv_ref, hv_ref)
    o_ref[...] = (jax.nn.gelu(gate, approximate=True) * val).astype(o_ref.dtype)


def ffn_up(h, w_up, conv_w, tm=1024, tn=512):
    s, d = h.shape
    f = w_up.shape[1] // 2
    nj = f // tn
    return pl.pallas_call(
        _ffn_up_kernel,
        grid=(nj, s // tm),
        in_specs=[pl.BlockSpec((tm, d), lambda j, i: (i, 0)),
                  pl.BlockSpec((d, tn), lambda j, i: (0, j)),
                  pl.BlockSpec((d, tn), lambda j, i: (0, j + nj)),
                  pl.BlockSpec((FFN_CONV, tn), lambda j, i: (0, j)),
                  pl.BlockSpec((FFN_CONV, tn), lambda j, i: (0, j + nj))],
        out_specs=pl.BlockSpec((tm, tn), lambda j, i: (i, j)),
        out_shape=jax.ShapeDtypeStruct((s, f), BF16),
        scratch_shapes=[pltpu.VMEM((SUBLANES, tn), F32), pltpu.VMEM((SUBLANES, tn), F32)],
        compiler_params=_params("parallel", "arbitrary"),
        name="ffn_up",
    )(h, w_up, w_up, conv_w, conv_w)


def _dot_nt(a, b):
    return lax.dot_general(a, b, (((1,), (1,)), ((), ())), preferred_element_type=F32)


def _moba_kernel(slope_ref, q_ref, k_ref, v_ref, o_ref, kbar_ref, vt_ref,
                 s0_buf, s1_buf, p0_buf, p1_buf, *, n_blk):
    hd = pl.program_id(0)
    g = pl.program_id(1)
    bs = MOBA_BLOCK
    slope = slope_ref[hd]

    def pair_rows(ref, t):
        return ref[pl.ds(pl.multiple_of(t * (2 * bs), 2 * bs), 2 * bs), :]

    @pl.when(g == 0)
    def _():
        def kmean(j, c):
            kb = k_ref[pl.ds(pl.multiple_of(j * bs, bs), bs), :].astype(F32)
            kbar_ref[pl.ds(j, 1), :] = jnp.mean(kb, axis=0, keepdims=True)
            return c
        lax.fori_loop(0, n_blk, kmean, 0)

        def vtrans(t, c):
            vt_ref[t] = pair_rows(v_ref, t).astype(F32).T.astype(BF16)
            return c
        lax.fori_loop(0, n_blk // 2, vtrans, 0)

    q = q_ref[...]
    second = lax.broadcasted_iota(jnp.int32, (1, 2 * bs), 1) >= bs
    own = 2 * g + second.astype(jnp.int32)
    kbar = kbar_ref[...]
    kb_hi = kbar.astype(BF16)
    kb_lo = (kbar - kb_hi.astype(F32)).astype(BF16)
    gate = _dot_nt(kb_hi, q) + _dot_nt(kb_lo, q)
    blk = lax.broadcasted_iota(jnp.int32, gate.shape, 0)
    gate = jnp.where(blk < own, gate, -jnp.inf)
    sel = []
    for _ in range(MOBA_TOPK):
        m = jnp.max(gate, axis=0, keepdims=True)
        cand = (gate == m) & (gate > -jnp.inf)
        idx = jnp.min(jnp.where(cand, blk, n_blk), axis=0, keepdims=True)
        sel.append(idx)
        gate = jnp.where(blk == idx, -jnp.inf, gate)

    slope2 = slope * LOG2E
    key = lax.broadcasted_iota(jnp.int32, (2 * bs, 2 * bs), 0)
    qry = lax.broadcasted_iota(jnp.int32, (2 * bs, 2 * bs), 1)
    cb_pair = slope2 * key.astype(F32)

    def qbias(j):
        picked = (sel[0] == j) | (sel[1] == j) | (sel[2] == j)
        return jnp.where(picked, 0.0, NEG_BIG)

    s = _dot_nt(pair_rows(k_ref, g), q) + cb_pair
    ok_first = jnp.where(second, qbias(2 * g), 0.0)
    s = jnp.where(key <= qry, s, NEG_BIG)
    s_a, s_b = s[:bs] + ok_first, s[bs:]
    m0 = jnp.maximum(jnp.max(s_a, axis=0, keepdims=True), jnp.max(s_b, axis=0, keepdims=True))
    p = jnp.exp2(jnp.concatenate([s_a, s_b], axis=0) - m0)
    l0 = jnp.sum(p, axis=0, keepdims=True)
    acc0 = jnp.dot(vt_ref[g], p.astype(BF16), preferred_element_type=F32)

    n_it = g
    t_last = jnp.maximum(n_it - 1, 0)

    s_buf = (s0_buf, s1_buf)
    p_buf = (p0_buf, p1_buf)
    s_buf[0][...] = _dot_nt(pair_rows(k_ref, 0), q)
    p_buf[1][...] = jnp.zeros((2 * bs, 2 * bs), BF16)

    def phase(t, carry, cur):
        m_i, l_i, acc = carry
        s_buf[1 - cur][...] = _dot_nt(pair_rows(k_ref, jnp.minimum(t + 1, t_last)), q)
        pv = jnp.dot(vt_ref[jnp.clip(t - 1, 0, t_last)], p_buf[1 - cur][...],
                     preferred_element_type=F32)
        ja = 2 * t
        s = s_buf[cur][...] + cb_pair
        sa, sb = s[:bs], s[bs:]
        off = jnp.where(t < n_it, slope2 * ((ja - 2 * g) * bs).astype(F32), NEG_BIG)
        qa, qb = qbias(ja) + off, qbias(ja + 1) + off
        m_new = jnp.maximum(m_i, jnp.maximum(jnp.max(sa, axis=0, keepdims=True) + qa,
                                             jnp.max(sb, axis=0, keepdims=True) + qb))
        pa = jnp.exp2(sa + (qa - m_new))
        pb = jnp.exp2(sb + (qb - m_new))
        alpha = jnp.exp2(m_i - m_new)
        l_new = (alpha * l_i + jnp.sum(pa, axis=0, keepdims=True)
                 + jnp.sum(pb, axis=0, keepdims=True))
        p_buf[cur][...] = jnp.concatenate([pa, pb], axis=0).astype(BF16)
        return m_new, l_new, alpha * (acc + pv)

    def body(u, carry):
        for ph in range(MOBA_PHASES):
            carry = phase(MOBA_PHASES * u + ph, carry, ph % 2)
        return carry

    n_trip = (n_it + MOBA_PHASES - 1) // MOBA_PHASES
    _, l_f, acc_f = lax.fori_loop(0, n_trip, body, (m0, l0, acc0))
    acc_f = acc_f + jnp.dot(vt_ref[jnp.clip(MOBA_PHASES * n_trip - 1, 0, t_last)], p_buf[1][...],
                            preferred_element_type=F32)
    o_ref[...] = (acc_f / l_f).T.astype(o_ref.dtype)


def moba_attention(proj, n_heads):
    s = proj.shape[0]
    dh, bs = MOBA_HEAD_DIM, MOBA_BLOCK
    n_blk = s // bs
    slopes = 2.0 ** (-8.0 * jnp.arange(1, n_heads + 1, dtype=F32) / n_heads)
    grid_spec = pltpu.PrefetchScalarGridSpec(
        num_scalar_prefetch=1,
        grid=(n_heads, n_blk // 2),
        in_specs=[pl.BlockSpec((2 * bs, dh), lambda h, g, sl: (g, h)),
                  pl.BlockSpec((s, dh), lambda h, g, sl: (0, n_heads + h)),
                  pl.BlockSpec((s, dh), lambda h, g, sl: (0, 2 * n_heads + h))],
        out_specs=pl.BlockSpec((2 * bs, dh), lambda h, g, sl: (g, h)),
        scratch_shapes=[pltpu.VMEM((n_blk, dh), F32),
                        pltpu.VMEM((n_blk // 2, dh, 2 * bs), BF16),
                        pltpu.VMEM((2 * bs, 2 * bs), F32), pltpu.VMEM((2 * bs, 2 * bs), F32),
                        pltpu.VMEM((2 * bs, 2 * bs), BF16), pltpu.VMEM((2 * bs, 2 * bs), BF16)],
    )
    return pl.pallas_call(
        functools.partial(_moba_kernel, n_blk=n_blk),
        grid_spec=grid_spec,
        out_shape=jax.ShapeDtypeStruct((s, n_heads * dh), BF16),
        compiler_params=_params("parallel", "arbitrary"),
        name="moba",
    )(slopes, proj, proj, proj)


def _softplus(z):
    return jnp.maximum(z, 0.0) + jnp.log1p(jnp.exp(-jnp.abs(z)))


def _sb_kernel(q_ref, k_ref, v_ref, o_ref, z0_buf, z1_buf, w0_buf, w1_buf, *, tb):
    i = pl.program_id(1)
    q = q_ref[...]
    row = lax.broadcasted_iota(jnp.int32, (tb, tb), 0)
    col = lax.broadcasted_iota(jnp.int32, (tb, tb), 1)
    tri = jnp.where(row >= col, 1.0, 0.0).astype(BF16)
    past = col < row

    def weights(z2, later, masked):
        nz = -z2
        lk = jnp.minimum(nz, 0.0) - jnp.log2(1.0 + jnp.exp2(jnp.minimum(z2, nz)))
        if masked:
            lk = jnp.where(past, lk, 0.0)
        c = jnp.dot(lk.astype(BF16), tri, preferred_element_type=F32)
        w = jnp.exp2(z2 + c + later)
        if masked:
            w = jnp.where(past, w, 0.0)
        return w.astype(BF16), jnp.sum(lk, axis=-1, keepdims=True)

    i0 = pl.multiple_of(i * tb, tb)
    w0, later0 = weights(_dot_nt(q, k_ref[pl.ds(i0, tb), :]), jnp.zeros((tb, 1), F32), True)
    acc0 = jnp.dot(w0, v_ref[pl.ds(i0, tb), :], preferred_element_type=F32)

    n_pair = i // 2
    t_last = jnp.maximum(n_pair - 1, 0)
    z_buf = (z0_buf, z1_buf)
    w_buf = (w0_buf, w1_buf)

    def pair_rows(ref, t):
        j0 = jnp.maximum(i - 2 - 2 * t, 0) * tb
        return ref[pl.ds(pl.multiple_of(j0, tb), 2 * tb), :]

    z_buf[0][...] = _dot_nt(q, pair_rows(k_ref, 0))
    w_buf[1][...] = jnp.zeros((tb, 2 * tb), BF16)

    def pair(t, carry, cur):
        acc, later = carry
        z_buf[1 - cur][...] = _dot_nt(q, pair_rows(k_ref, jnp.minimum(t + 1, t_last)))
        acc = acc + jnp.dot(w_buf[1 - cur][...], pair_rows(v_ref, jnp.clip(t - 1, 0, t_last)),
                            preferred_element_type=F32)
        z2 = z_buf[cur][...]
        valid = t < n_pair
        later_hi = jnp.where(valid, later, NEG_BIG)
        w_hi, tot_hi = weights(z2[:, tb:], later_hi, False)
        w_lo, tot_lo = weights(z2[:, :tb], later_hi + tot_hi, False)
        w_buf[cur][...] = jnp.concatenate([w_lo, w_hi], axis=1)
        return acc, later + jnp.where(valid, tot_hi + tot_lo, 0.0)

    def trips(first, n):
        def body(u, carry):
            for ph in range(n):
                carry = pair(first + n * u + ph, carry, ph % 2)
            return carry
        return body

    n_long = n_pair // SB_PAIRS
    done = n_long * SB_PAIRS
    n_short = (n_pair - done + 1) // 2
    carry = lax.fori_loop(0, n_long, trips(0, SB_PAIRS), (acc0, later0))
    acc, later = lax.fori_loop(0, n_short, trips(done, 2), carry)
    acc = acc + jnp.dot(w_buf[1][...], pair_rows(v_ref, jnp.clip(done + 2 * n_short - 1, 0, t_last)),
                        preferred_element_type=F32)

    def single(t, carry):
        acc, later = carry
        w, tot = weights(_dot_nt(q, k_ref[pl.ds(0, tb), :]), later, False)
        return acc + jnp.dot(w, v_ref[pl.ds(0, tb), :], preferred_element_type=F32), later + tot

    acc, _ = lax.fori_loop(0, i - 2 * n_pair, single, (acc, later))
    o_ref[...] = acc.astype(o_ref.dtype)


def sb_attention(proj, n_heads, col0, tb=256):
    s = proj.shape[0]
    dh = SB_HEAD_DIM
    c0 = col0 // dh
    return pl.pallas_call(
        functools.partial(_sb_kernel, tb=tb),
        grid=(n_heads, s // tb),
        in_specs=[pl.BlockSpec((tb, dh), lambda h, i: (i, c0 + h)),
                  pl.BlockSpec((s, dh), lambda h, i: (0, c0 + n_heads + h)),
                  pl.BlockSpec((s, dh), lambda h, i: (0, c0 + 2 * n_heads + h))],
        out_specs=pl.BlockSpec((tb, dh), lambda h, i: (i, h)),
        out_shape=jax.ShapeDtypeStruct((s, n_heads * dh), BF16),
        scratch_shapes=[pltpu.VMEM((tb, 2 * tb), F32), pltpu.VMEM((tb, 2 * tb), F32),
                        pltpu.VMEM((tb, 2 * tb), BF16), pltpu.VMEM((tb, 2 * tb), BF16)],
        compiler_params=_params("parallel", "arbitrary"),
        name="stickbreak",
    )(proj, proj, proj)


RWKV_W = 2048
LORA_W = 640
HEAD_GROUP = 256
RWKV_CHUNK = 64


def _sigmoid(x):
    return 1.0 / (1.0 + jnp.exp(-x))


def _head_ones(n):
    r = lax.broadcasted_iota(jnp.int32, (n, n), 0) // RWKV_HEAD
    c = lax.broadcasted_iota(jnp.int32, (n, n), 1) // RWKV_HEAD
    return jnp.where(r == c, 1.0, 0.0).astype(BF16)


def _head_sum(x):
    e = _head_ones(HEAD_GROUP)
    hi = x.astype(BF16)
    lo = (x - hi.astype(F32)).astype(BF16)
    outs = []
    for c in range(x.shape[1] // HEAD_GROUP):
        sl = slice(c * HEAD_GROUP, (c + 1) * HEAD_GROUP)
        outs.append(jnp.dot(hi[:, sl], e, preferred_element_type=F32)
                    + jnp.dot(lo[:, sl], e, preferred_element_type=F32))
    return jnp.concatenate(outs, axis=-1)


def _token_shift(x_ref, halo_ref, mu_ref, first):
    x = x_ref[...]
    prev = jnp.where(first, 0.0, halo_ref[...])
    return x + (_shift_rows(x, prev, 1) - x) * mu_ref[...]


def _rwkv_prep_kernel(*refs, vres):
    (r_ref, k_ref, v_ref, lo_ref, rh_ref, kh_ref, vh_ref, loh_ref,
     mur_ref, muk_ref, muv_ref, mulo_ref,
     w0_ref, w2_ref, a0_ref, a2_ref, v0_ref, v2_ref, g2_ref, kk_ref, ka_ref, rk_ref) = refs[:22]
    rest = refs[22:]
    if vres:
        vf_ref, rest = rest[0], rest[1:]
    ro_ref, lw_ref, ko_ref, vo_ref, na_ref, bb_ref, g_ref, bo_ref = rest
    first = pl.program_id(0) == 0
    r = _token_shift(r_ref, rh_ref, mur_ref, first)
    k = _token_shift(k_ref, kh_ref, muk_ref, first)
    v = _token_shift(v_ref, vh_ref, muv_ref, first)
    lo = _token_shift(lo_ref, loh_ref, mulo_ref, first)
    w_lo, a_lo, v_lo, g_lo = lo[:, 0:128], lo[:, 128:256], lo[:, 256:384], lo[:, 384:640]

    def lora(x, w_ref):
        return jnp.dot(x.astype(BF16), w_ref[...], preferred_element_type=F32)

    w_log = -_softplus(-(w0_ref[...] + lora(jnp.tanh(w_lo), w2_ref))) - 0.5
    lw_ref[...] = -jnp.exp(w_log)
    if vres:
        v = v + (vf_ref[...] - v) * _sigmoid(v0_ref[...] + lora(v_lo, v2_ref))
    a = _sigmoid(a0_ref[...] + lora(a_lo, a2_ref))
    g_ref[...] = lora(_sigmoid(g_lo), g2_ref)
    kk = k * kk_ref[...]
    kk = kk / jnp.maximum(jnp.sqrt(_head_sum(kk * kk)), 1e-12)
    k2 = k * (1.0 + (a - 1.0) * ka_ref[...])
    ro_ref[...] = r
    ko_ref[...] = k2
    vo_ref[...] = v
    na_ref[...] = -kk
    bb_ref[...] = kk * a
    bo_ref[...] = _head_sum(r * k2 * rk_ref[...]) * v


def rwkv_prep(proj, col0, mu, w0, w2p, a0, a2p, v0, v2p, g2, k_k, k_a, r_k, v_first, tm=128):
    s = proj.shape[0]
    w = RWKV_W
    vres = v_first is not None
    c_r, c_lo = col0 // w, (col0 + 3 * w) // LORA_W
    hb = tm // SUBLANES

    def main(width, cidx):
        return pl.BlockSpec((tm, width), lambda i: (i, cidx))

    def halo(width, cidx):
        return pl.BlockSpec((SUBLANES, width), lambda i: (jnp.maximum(i * hb - 1, 0), cidx))

    def vec(width, cidx=0):
        return pl.BlockSpec((1, width), lambda i: (0, cidx))

    def full(shape):
        return pl.BlockSpec(shape, lambda i: (0, 0))

    mu2 = mu.reshape(1, -1)
    r1 = lambda t: t.reshape(1, -1)
    in_specs = [main(w, c_r), main(w, c_r + 1), main(w, c_r + 2), main(LORA_W, c_lo),
                halo(w, c_r), halo(w, c_r + 1), halo(w, c_r + 2), halo(LORA_W, c_lo),
                vec(w, c_r), vec(w, c_r + 1), vec(w, c_r + 2), vec(LORA_W, c_lo),
                vec(w), full((128, w)), vec(w), full((128, w)), vec(w), full((128, w)),
                full((256, w)), vec(w), vec(w), vec(w)]
    args = [proj] * 8 + [mu2] * 4 + [r1(w0), w2p, r1(a0), a2p, r1(v0), v2p, g2, r1(k_k), r1(k_a), r1(r_k)]
    if vres:
        in_specs.append(main(w, 0))
        args.append(v_first)
    out = pl.BlockSpec((tm, w), lambda i: (i, 0))
    return pl.pallas_call(
        functools.partial(_rwkv_prep_kernel, vres=vres),
        grid=(s // tm,),
        in_specs=in_specs,
        out_specs=[out] * 8,
        out_shape=[jax.ShapeDtypeStruct((s, w), F32)] * 8,
        compiler_params=_params("parallel"),
        name="rwkv_prep",
    )(*args)


def _cumsum_rows(x):
    n = x.shape[0]
    row = lax.broadcasted_iota(jnp.int32, x.shape, 0)
    d = 1
    while d < n:
        x = x + jnp.where(row >= d, pltpu.roll(x, d, 0), 0.0)
        d *= 2
    return x


def _rwkv_chunk(r, lw, k, v, a, b, state):
    L, n2 = r.shape
    nh = n2 // RWKV_HEAD
    J = nh * L
    bd = lambda x: x.astype(BF16)
    mm = lambda x, y: jnp.dot(bd(x), bd(y), preferred_element_type=F32)
    lg = _cumsum_rows(lw)
    tot = lg[L - 1:L, :]
    e_neg = jnp.exp(-lg)
    e_hat = jnp.exp(tot - lg)
    a_t = a * jnp.exp(lg - lw)
    r_t = r * jnp.exp(lg)
    b_i, k_i = b * e_neg, k * e_neg
    b_hat, k_hat = b * e_hat, k * e_hat

    head = lax.broadcasted_iota(jnp.int32, (L, n2), 1) // RWKV_HEAD

    def stack(x):
        return jnp.concatenate([jnp.where(head == h, x, 0.0) for h in range(nh)], axis=0)

    def fold(x):
        out = x[:L]
        for h in range(1, nh):
            out = out + x[h * L:(h + 1) * L]
        return out

    row = lax.broadcasted_iota(jnp.int32, (J, J), 0)
    col = lax.broadcasted_iota(jnp.int32, (J, J), 1)
    same = (row // L) == (col // L)
    strict = same & (col < row)
    incl = same & (col <= row)
    eye = jnp.where(row == col, 1.0, 0.0)

    x_a, x_r = stack(a_t), stack(r_t)
    ybk = jnp.concatenate([b_i] * nh + [k_i] * nh, axis=0)
    pa = _dot_nt(bd(x_a), bd(ybk))
    pr = _dot_nt(bd(x_r), bd(ybk))
    yield
    m_ab = jnp.where(strict, pa[:, :J], 0.0)
    m_ak = jnp.where(strict, pa[:, J:], 0.0)
    m_rb = jnp.where(incl, pr[:, :J], 0.0)
    m_rk = jnp.where(incl, pr[:, J:], 0.0)

    v_st = stack(v)
    w1 = mm(m_ak, v_st)
    t_inv, pw = eye + m_ab, m_ab
    for _ in range(int(math.log2(L)) - 1):
        pw = mm(pw, pw)
        yield
        t_inv = mm(t_inv, eye + pw)

    yield
    tw = mm(t_inv, jnp.concatenate([x_a, w1], axis=1))
    yield
    ta, w0 = fold(tw[:, :n2]), fold(tw[:, n2:])
    gs = _dot_nt(bd(jnp.concatenate([ta, r_t], axis=0)), bd(state))
    lhs_t = jnp.concatenate([jnp.concatenate([ta, jnp.zeros_like(ta)], axis=0).T,
                             jnp.concatenate([w0, v], axis=0).T], axis=0)
    pq = mm(lhs_t, jnp.concatenate([b_hat, k_hat], axis=0))
    yield
    u = gs[:L] + w0
    yj = mm(jnp.concatenate([m_rb, m_rk], axis=1), jnp.concatenate([stack(u), v_st], axis=0))
    y = gs[L:] + fold(yj)
    yield
    ch_r = lax.broadcasted_iota(jnp.int32, (n2, n2), 0)
    ch_c = lax.broadcasted_iota(jnp.int32, (n2, n2), 1)
    head_blk = (ch_r // RWKV_HEAD) == (ch_c // RWKV_HEAD)
    p_mat = jnp.where(head_blk, pq[:n2], 0.0) + jnp.where(ch_r == ch_c, jnp.exp(tot), 0.0)
    q_mat = jnp.where(head_blk, pq[n2:], 0.0)
    return y, mm(state, p_mat) + q_mat


def _run_lockstep(chains):
    results = [None] * len(chains)
    live = list(range(len(chains)))
    while live:
        for g in list(live):
            try:
                next(chains[g])
            except StopIteration as done:
                results[g] = done.value
                live.remove(g)
    return results


def _rwkv_rec_kernel(r_ref, lw_ref, k_ref, v_ref, a_ref, b_ref, y_ref, st_ref, *, n_groups, n_chunks):
    gw = st_ref.shape[-1]

    @pl.when(pl.program_id(1) == 0)
    def _():
        st_ref[...] = jnp.zeros_like(st_ref)

    def body(c, carry):
        rows = pl.ds(pl.multiple_of(c * RWKV_CHUNK, RWKV_CHUNK), RWKV_CHUNK)
        in_refs = (r_ref, lw_ref, k_ref, v_ref, a_ref, b_ref)
        chains = [_rwkv_chunk(*[ref[rows, p * gw:(p + 1) * gw] for ref in in_refs], st_ref[p])
                  for p in range(n_groups)]
        for p, (y, st) in enumerate(_run_lockstep(chains)):
            y_ref[rows, p * gw:(p + 1) * gw] = y
            st_ref[p] = st
        return carry

    lax.fori_loop(0, n_chunks, body, 0)


def rwkv_recurrence(r, lw, k, v, a, b, tt=512, n_groups=4, heads_per_group=4):
    s, w = r.shape
    gw = heads_per_group * RWKV_HEAD
    wb = n_groups * gw
    blk = pl.BlockSpec((tt, wb), lambda h, t: (t, h))
    return pl.pallas_call(
        functools.partial(_rwkv_rec_kernel, n_groups=n_groups, n_chunks=tt // RWKV_CHUNK),
        grid=(w // wb, s // tt),
        in_specs=[blk] * 6,
        out_specs=blk,
        out_shape=jax.ShapeDtypeStruct((s, w), F32),
        scratch_shapes=[pltpu.VMEM((n_groups, gw, gw), F32)],
        compiler_params=_params("arbitrary", "arbitrary"),
        name="rwkv_rec",
    )(r, lw, k, v, a, b)


def _rwkv_post_kernel(y_ref, g_ref, bo_ref, lw_ref, lb_ref, o_ref):
    y = y_ref[...]
    inv_n = 1.0 / RWKV_HEAD
    mu = _head_sum(y) * inv_n
    yc = y - mu
    var = _head_sum(yc * yc) * inv_n
    yn = yc * lax.rsqrt(var + GN_EPS) * lw_ref[...] + lb_ref[...]
    o_ref[...] = ((yn + bo_ref[...]) * g_ref[...]).astype(o_ref.dtype)


def rwkv_post(y, g, bonus, lnx_w, lnx_b, tm=256):
    s, w = y.shape
    row = pl.BlockSpec((tm, w), lambda i: (i, 0))
    vec = pl.BlockSpec((1, w), lambda i: (0, 0))
    return pl.pallas_call(
        _rwkv_post_kernel,
        grid=(s // tm,),
        in_specs=[row, row, row, vec, vec],
        out_specs=row,
        out_shape=jax.ShapeDtypeStruct((s, w), BF16),
        compiler_params=_params("parallel"),
        name="rwkv_post",
    )(y, g, bonus, lnx_w.reshape(1, w), lnx_b.reshape(1, w))


def _expm1(y):
    u = jnp.exp(y)
    near = (u - 1.0) * y / jnp.log(u)
    return jnp.where(u == 1.0, y, jnp.where(y < -0.5, u - 1.0, near))


def _lru_kernel(gate_ref, x_ref, xh_ref, cw_ref, cb_ref, gw_ref, gb_ref, lam_ref, o_ref, h_ref, *, nb):
    t = pl.program_id(1)

    @pl.when(t == 0)
    def _():
        h_ref[...] = jnp.zeros_like(h_ref)

    x = x_ref[...]
    prev = jnp.where(t == 0, 0.0, xh_ref[...])
    cw = cw_ref[...]
    xb = cw[3:4, :] * x + cb_ref[...]
    for d in range(1, LRU_CONV):
        xb = xb + cw[3 - d:4 - d, :] * _shift_rows(x, prev, d)
    xb16 = xb.astype(BF16)
    g0, g1 = [], []
    for n in range(nb):
        xs = xb16[:, n * LANES:(n + 1) * LANES]
        g0.append(jnp.dot(xs, gw_ref[0, n], preferred_element_type=F32))
        g1.append(jnp.dot(xs, gw_ref[1, n], preferred_element_type=F32))
    gb = gb_ref[...]
    g0 = jnp.concatenate(g0, axis=-1) + gb[0:1, :]
    g1 = jnp.concatenate(g1, axis=-1) + gb[1:2, :]
    log_a = -LRU_C * _sigmoid(g0) * _softplus(-lam_ref[...])
    a = jnp.exp(log_a)
    b = jnp.sqrt(-_expm1(2.0 * log_a)) * _sigmoid(g1) * xb
    n = x.shape[0]
    row = lax.broadcasted_iota(jnp.int32, x.shape, 0)
    d = 1
    while d < n:
        keep = row >= d
        a_s = jnp.where(keep, pltpu.roll(a, d, 0), 1.0)
        b_s = jnp.where(keep, pltpu.roll(b, d, 0), 0.0)
        b = a * b_s + b
        a = a * a_s
        d *= 2
    h = b + a * h_ref[SUBLANES - 1:SUBLANES, :]
    h_ref[...] = h[n - SUBLANES:, :]
    o_ref[...] = (h * jax.nn.gelu(gate_ref[...], approximate=True)).astype(o_ref.dtype)


def rg_lru(proj, conv_w, conv_b, gate_w, gate_b, lam, tm=256, tc=512):
    s = proj.shape[0]
    w = conv_w.shape[1]
    nj = w // tc
    nb = tc // LANES
    hb = tm // SUBLANES
    return pl.pallas_call(
        functools.partial(_lru_kernel, nb=nb),
        grid=(nj, s // tm),
        in_specs=[pl.BlockSpec((tm, tc), lambda j, t: (t, j)),
                  pl.BlockSpec((tm, tc), lambda j, t: (t, nj + j)),
                  pl.BlockSpec((SUBLANES, tc), lambda j, t: (jnp.maximum(t * hb - 1, 0), nj + j)),
                  pl.BlockSpec((LRU_CONV, tc), lambda j, t: (0, j)),
                  pl.BlockSpec((1, tc), lambda j, t: (0, j)),
                  pl.BlockSpec((2, nb, LANES, LANES), lambda j, t: (0, j, 0, 0)),
                  pl.BlockSpec((2, tc), lambda j, t: (0, j)),
                  pl.BlockSpec((1, tc), lambda j, t: (0, j))],
        out_specs=pl.BlockSpec((tm, tc), lambda j, t: (t, j)),
        out_shape=jax.ShapeDtypeStruct((s, w), BF16),
        scratch_shapes=[pltpu.VMEM((SUBLANES, tc), F32)],
        compiler_params=_params("arbitrary", "arbitrary"),
        name="rg_lru",
    )(proj, proj, proj, conv_w, conv_b.reshape(1, w), gate_w.astype(BF16), gate_b, lam.reshape(1, w))


def even_mixer(h, w_in, w_out):
    half = w_out.shape[0] // 2
    n_moba = half // MOBA_HEAD_DIM
    n_sb = half // SB_HEAD_DIM
    col = jnp.arange(w_in.shape[1])
    qscale = jnp.where(col < half, MOBA_HEAD_DIM ** -0.5 * LOG2E,
                       jnp.where((col >= 3 * half) & (col < 4 * half), SB_HEAD_DIM ** -0.5 * LOG2E, 1.0))
    proj = matmul(h, (w_in * qscale.astype(F32)).astype(BF16), BF16)
    oa = moba_attention(proj, n_moba)
    ob = sb_attention(proj, n_sb, 3 * n_moba * MOBA_HEAD_DIM)
    return matmul_cat(oa, ob, w_out.astype(BF16), F32)


def _pad_cols(t, width):
    return jnp.pad(t, [(0, 0)] * (t.ndim - 1) + [(0, width - t.shape[-1])])


def _odd_layout(t, n_c, vres):
    w = RWKV_W
    sizes = [96, 96] + ([64] if vres else []) + [256]
    segs, c = [], 3 * w
    for sz in sizes:
        segs.append(_pad_cols(t[..., c:c + sz], LANES if sz < LANES else sz))
        c += sz
    if not vres:
        segs.insert(2, jnp.zeros(t.shape[:-1] + (LANES,), t.dtype))
    assert c == n_c
    return jnp.concatenate([t[..., n_c:], t[..., :3 * w]] + segs, axis=-1)


def _pad_rows(t, rows):
    return jnp.pad(t, ((0, rows - t.shape[0]), (0, 0)))


def odd_mixer(h, v_first, p):
    w = RWKV_W
    vres = "v2" in p
    n_c = p["shift_mu"].shape[0]
    w_in = _odd_layout(p["w_in"], n_c, vres).astype(BF16)
    lru_w = p["w_in"].shape[1] - n_c
    mu = _odd_layout(jnp.concatenate([p["shift_mu"], jnp.zeros((lru_w,), F32)]), n_c, vres)
    proj = matmul(h, w_in, F32, tn=LORA_W)
    zeros_w = jnp.zeros((w,), F32)
    r, lw, k, v, na, bb, g, bonus = rwkv_prep(
        proj, lru_w, mu, p["w0"], _pad_rows(p["w2"], LANES).astype(BF16),
        p["a0"], _pad_rows(p["a2"], LANES).astype(BF16),
        p["v0"] if vres else zeros_w,
        _pad_rows(p["v2"], LANES).astype(BF16) if vres else jnp.zeros((LANES, w), BF16),
        p["g2"].astype(BF16), p["k_k"], p["k_a"], p["r_k"].reshape(-1), v_first)
    if v_first is None:
        v_first = v
    y = rwkv_recurrence(r, lw, k, v, na, bb)
    c_out = rwkv_post(y, g, bonus, p["lnx_w"], p["lnx_b"])
    d_out = rg_lru(proj, p["conv_w"], p["conv_b"], p["gate_w"], p["gate_b"], p["lru_lambda"])
    return matmul_cat(c_out, d_out, p["w_out"].astype(BF16), F32), v_first


_ODD_NAMES = ("w_in", "w_out", "shift_mu", "w0", "w2", "a0", "a2", "g2", "k_k", "k_a", "r_k",
              "lnx_w", "lnx_b", "conv_w", "conv_b", "gate_w", "gate_b", "lru_lambda")


def _forward(x, layers):
    x2 = x.reshape(x.shape[-2], x.shape[-1])
    h = rmsnorm(x2, layers[0]["norms"][0])
    v_first = None
    for li, p in enumerate(layers):
        g = p["norms"]
        if "shift_mu" in p:
            mix, v_first = odd_mixer(h, v_first, p)
        else:
            mix = even_mixer(h, p["w_in"], p["w_out"])
        x2, h = add_rmsnorm(x2, mix, g[1], g[2])
        act = ffn_up(h, p["ffn_up"].astype(BF16), p["ffn_conv"])
        y = matmul(act, p["ffn_down"].astype(BF16), F32)
        g_next = layers[li + 1]["norms"][0] if li + 1 < len(layers) else g[3]
        x2, h = add_rmsnorm(x2, y, g[3], g_next)
    return x2.reshape(x.shape)


def kernel(x, l0_norms, l0_w_in, l0_w_out, l0_ffn_up, l0_ffn_conv, l0_ffn_down, l1_norms, l1_w_in, l1_w_out, l1_shift_mu, l1_w0, l1_w2, l1_a0, l1_a2, l1_g2, l1_k_k, l1_k_a, l1_r_k, l1_lnx_w, l1_lnx_b, l1_conv_w, l1_conv_b, l1_gate_w, l1_gate_b, l1_lru_lambda, l1_ffn_up, l1_ffn_conv, l1_ffn_down, l2_norms, l2_w_in, l2_w_out, l2_ffn_up, l2_ffn_conv, l2_ffn_down, l3_norms, l3_w_in, l3_w_out, l3_shift_mu, l3_w0, l3_w2, l3_a0, l3_a2, l3_v0, l3_v2, l3_g2, l3_k_k, l3_k_a, l3_r_k, l3_lnx_w, l3_lnx_b, l3_conv_w, l3_conv_b, l3_gate_w, l3_gate_b, l3_lru_lambda, l3_ffn_up, l3_ffn_conv, l3_ffn_down):
    a = locals()
    layers = []
    for li in range(4):
        pre = "l%d_" % li
        layers.append({k[len(pre):]: v for k, v in a.items() if k.startswith(pre)})
    return _forward(x, layers)
```
